```python
import math, functools
import jax, jax.numpy as jnp
from jax import lax
import numpy as np

D_MODEL = 1024
BATCH = 2
SEQ = 16384
DEPTH = 4
DEC_BATCH = 16
DEC_SEQ = 64
PAST_LEN = 2048

CHUNK = 64
QBLOCK = 128
KBLOCK = 128
MIX_WIDTH = D_MODEL
IN_WIDTH = 4 * MIX_WIDTH
DIFF_HEADS = 4
DIFF_DH = 128
SB_HEADS = 4
SB_DH = 256
KV_WIDTH = MIX_WIDTH
ROPE_THETA = 10000.0
EPS = 1e-6
NEG_INF = -1e30
N_DIFF_LAYERS = (DEPTH + 1) // 2

kernel_name = "diff_stickbreak_streaming_step"


def _rmsnorm(x, g):
    xf = x.astype(jnp.float32)
    y = xf * lax.rsqrt(jnp.mean(xf * xf, axis=-1, keepdims=True) + EPS) * g.astype(jnp.float32)
    return y.astype(x.dtype)


def _rope(x, pos):
    half = x.shape[-1] // 2
    inv = ROPE_THETA ** (-jnp.arange(half, dtype=jnp.float32) / half)
    ang = pos.astype(jnp.float32)[:, None] * inv[None, :]
    cos = jnp.cos(ang)[None, :, None, :]
    sin = jnp.sin(ang)[None, :, None, :]
    xf = x.astype(jnp.float32)
    x1, x2 = xf[..., :half], xf[..., half:]
    return jnp.concatenate([x1 * cos - x2 * sin, x2 * cos + x1 * sin], axis=-1).astype(x.dtype)


def _rev_cumsum(x):
    S = x.shape[-1]
    nkb = -(-S // KBLOCK)
    pad = nkb * KBLOCK - S
    xp = jnp.pad(x, [(0, 0)] * (x.ndim - 1) + [(0, pad)])
    xb = xp.reshape(*x.shape[:-1], nkb, KBLOCK)
    ar = jnp.arange(KBLOCK)
    tri = (ar[:, None] >= ar[None, :]).astype(x.dtype)
    within = jnp.einsum('...nj,js->...ns', xb, tri, precision=lax.Precision.HIGHEST)
    tot = jnp.sum(xb, axis=-1)
    an = jnp.arange(nkb)
    btri = (an[:, None] > an[None, :]).astype(x.dtype)
    after = jnp.einsum('...m,mn->...n', tot, btri, precision=lax.Precision.HIGHEST)
    out = (within + after[..., None]).reshape(*x.shape[:-1], nkb * KBLOCK)
    return out[..., :S]


def _attend(block_fn, q, k_all, v_all, P):
    B, T = q.shape[0], q.shape[1]
    nb = T // QBLOCK if (T % QBLOCK == 0 and T > QBLOCK) else 1
    blk = T // nb
    outs = []
    for b in range(nb):
        end = P + (b + 1) * blk
        q_pos = P + b * blk + jnp.arange(blk, dtype=jnp.int32)
        k_pos = jnp.arange(end, dtype=jnp.int32)
        outs.append(block_fn(q[:, b * blk:(b + 1) * blk], q_pos,
                             k_all[:, :end], v_all[:, :end], k_pos))
    return outs[0] if nb == 1 else jnp.concatenate(outs, axis=1)


def _diff_block(q_blk, qpos_blk, k_all, v_all, kpos, lam):
    s = jnp.einsum('bqhmd,bkhmd->bhmqk', q_blk, k_all,
                   preferred_element_type=jnp.float32) * (DIFF_DH ** -0.5)
    visible = (kpos // CHUNK)[None, :] <= (qpos_blk // CHUNK)[:, None]
    s = jnp.where(visible, s, NEG_INF)
    p = jax.nn.softmax(s, axis=-1)
    w = p[:, :, 0] - lam * p[:, :, 1]
    return jnp.einsum('bhqk,bkhe->bqhe', w.astype(v_all.dtype), v_all)


def _sb_block(q_blk, qpos_blk, k_all, v_all, kpos):
    z = jnp.einsum('bqhd,bkhd->bhqk', q_blk, k_all,
                   preferred_element_type=jnp.float32) * (SB_DH ** -0.5)
    earlier = kpos[None, :] < qpos_blk[:, None]
    log_keep = jnp.where(earlier, jax.nn.log_sigmoid(-z), 0.0)
    a = jnp.where(earlier, jnp.exp(z + _rev_cumsum(log_keep)), 0.0)
    return jnp.einsum('bhqk,bkhd->bqhd', a.astype(v_all.dtype), v_all)


def _diff_mixer(h, past_k, past_v, w_in_l, w_out_l, lam_vecs, subln_w, lam_init):
    B, T, _ = h.shape
    P = 0 if past_k is None else past_k.shape[1]
    q_pos = P + jnp.arange(T, dtype=jnp.int32)
    q, k, v, g = jnp.split(h @ w_in_l, 4, axis=-1)
    q = _rope(q.reshape(B, T, 2 * DIFF_HEADS, DIFF_DH), q_pos).reshape(B, T, DIFF_HEADS, 2, DIFF_DH)
    k_rows = _rope(k.reshape(B, T, 2 * DIFF_HEADS, DIFF_DH), q_pos).reshape(B, T, KV_WIDTH)
    v_rows = v
    if past_k is None:
        k_all, v_all = k_rows, v_rows
    else:
        k_all = jnp.concatenate([past_k, k_rows], axis=1)
        v_all = jnp.concatenate([past_v, v_rows], axis=1)
    lf = lam_vecs.astype(jnp.float32)
    lam = jnp.exp(jnp.sum(lf[0] * lf[1])) - jnp.exp(jnp.sum(lf[2] * lf[3])) + lam_init
    block = functools.partial(_diff_block, lam=lam)
    o = _attend(block, q,
                k_all.reshape(B, P + T, DIFF_HEADS, 2, DIFF_DH),
                v_all.reshape(B, P + T, DIFF_HEADS, 2 * DIFF_DH), P)
    o = _rmsnorm(o, subln_w) * (1.0 - lam_init)
    o = o.reshape(B, T, MIX_WIDTH) * jax.nn.silu(g)
    return o @ w_out_l, k_rows, v_rows


def _sb_mixer(h, past_k, past_v, w_in_l, w_out_l):
    B, T, _ = h.shape
    P = 0 if past_k is None else past_k.shape[1]
    q, k_rows, v_rows, g = jnp.split(h @ w_in_l, 4, axis=-1)
    if past_k is None:
        k_all, v_all = k_rows, v_rows
    else:
        k_all = jnp.concatenate([past_k, k_rows], axis=1)
        v_all = jnp.concatenate([past_v, v_rows], axis=1)
    o = _attend(_sb_block, q.reshape(B, T, SB_HEADS, SB_DH),
                k_all.reshape(B, P + T, SB_HEADS, SB_DH),
                v_all.reshape(B, P + T, SB_HEADS, SB_DH), P)
    o = o.reshape(B, T, MIX_WIDTH) * jax.nn.silu(g)
    return o @ w_out_l, k_rows, v_rows


def setup_inputs(seed: int = 0) -> dict:
    key = jax.random.key(seed)
    ks = jax.random.split(key, 10)
    f32 = jnp.float32
    x_prompt = jax.random.normal(ks[0], (BATCH, SEQ, D_MODEL), f32)
    x_sample = jax.random.normal(ks[1], (DEC_BATCH, DEC_SEQ, D_MODEL), f32)
    cache_k = jax.random.normal(ks[2], (DEPTH, DEC_BATCH, PAST_LEN, KV_WIDTH), f32)
    cache_v = jax.random.normal(ks[3], (DEPTH, DEC_BATCH, PAST_LEN, KV_WIDTH), f32)
    norm_w = 1.0 + 0.01 * jax.random.normal(ks[4], (DEPTH, D_MODEL), f32)
    w_in = jax.random.normal(ks[5], (DEPTH, D_MODEL, IN_WIDTH), f32) * (D_MODEL ** -0.5)
    w_out = jax.random.normal(ks[6], (DEPTH, MIX_WIDTH, D_MODEL), f32) * (MIX_WIDTH ** -0.5)
    diff_lambda = 0.1 * jax.random.normal(ks[7], (N_DIFF_LAYERS, 4, DIFF_DH), f32)
    diff_subln_w = 1.0 + 0.01 * jax.random.normal(ks[8], (N_DIFF_LAYERS, 2 * DIFF_DH), f32)
    final_norm_w = 1.0 + 0.01 * jax.random.normal(ks[9], (D_MODEL,), f32)
    return {"x_prompt": x_prompt, "x_sample": x_sample, "cache_k": cache_k, "cache_v": cache_v,
            "norm_w": norm_w, "w_in": w_in, "w_out": w_out, "diff_lambda": diff_lambda,
            "diff_subln_w": diff_subln_w, "final_norm_w": final_norm_w}


def reference(x_prompt, x_sample, cache_k, cache_v, norm_w, w_in, w_out, diff_lambda,
              diff_subln_w, final_norm_w):
    yp, ys = x_prompt, x_sample
    kp_list, vp_list, ks_list, vs_list = [], [], [], []
    for i in range(DEPTH):
        hp = _rmsnorm(yp, norm_w[i])
        hs = _rmsnorm(ys, norm_w[i])
        if i % 2 == 0:
            j = i // 2
            lam_init = 0.8 - 0.6 * math.exp(-0.3 * i)
            mix = functools.partial(_diff_mixer, w_in_l=w_in[i], w_out_l=w_out[i],
                                    lam_vecs=diff_lambda[j], subln_w=diff_subln_w[j],
                                    lam_init=lam_init)
        else:
            mix = functools.partial(_sb_mixer, w_in_l=w_in[i], w_out_l=w_out[i])
        op, kp, vp = mix(hp, None, None)
        os_, ks_, vs_ = mix(hs, cache_k[i], cache_v[i])
        yp = yp + op
        ys = ys + os_
        kp_list.append(kp)
        vp_list.append(vp)
        ks_list.append(ks_)
        vs_list.append(vs_)
    y_prompt = _rmsnorm(yp, final_norm_w)
    y_sample = _rmsnorm(ys, final_norm_w)
    new_k_prompt = jnp.stack(kp_list, axis=0)
    new_v_prompt = jnp.stack(vp_list, axis=0)
    new_k_sample = jnp.stack(ks_list, axis=0)
    new_v_sample = jnp.stack(vs_list, axis=0)
    return (y_prompt, y_sample, new_k_prompt, new_v_prompt, new_k_sample, new_v_sample)
```

```python
import functools
import math

import jax
import jax.numpy as jnp
from jax import lax
from jax.experimental import pallas as pl
from jax.experimental.pallas import tpu as pltpu

F32 = jnp.float32
BF16 = jnp.bfloat16

CHUNK = 64
DIFF_HEADS = 4
DIFF_DH = 128
SB_HEADS = 4
SB_DH = 256
HEAD_W = 256
ROPE_THETA = 10000.0
EPS = 1e-6
NEG_INF = -1e30

V7X_VMEM_LIMIT = 56 * 1024 * 1024
ROW_TILE = 512
Q_TILE = 512
DIFF_K_TILE = 512
SB_K_TILE = 256


def _nt_dot(a, b):
    return lax.dot_general(a, b, (((1,), (1,)), ((), ())), preferred_element_type=F32)


def _dot(a, b):
    return jnp.dot(a, b, preferred_element_type=F32)


def _rms(x, w):
    return x * lax.rsqrt(jnp.mean(x * x, axis=-1, keepdims=True) + EPS) * w


def _rope_tile(x, cos, sin_signed):
    outs = []
    for g in range(x.shape[1] // DIFF_DH):
        xg = x[:, g * DIFF_DH:(g + 1) * DIFF_DH]
        outs.append(xg * cos + pltpu.roll(xg, DIFF_DH // 2, axis=1) * sin_signed)
    return jnp.concatenate(outs, axis=1)


def _proj_kernel(y_ref, nw_ref, w_ref, cos_ref, sin_ref,
                 q_ref, kf_ref, vf_ref, kb_ref, vb_ref, g_ref, *, rope, q_scale, width):
    h = _rms(y_ref[...], nw_ref[...]).astype(BF16)
    q = _dot(h, w_ref[:, 0 * width:1 * width])
    k = _dot(h, w_ref[:, 1 * width:2 * width])
    v = _dot(h, w_ref[:, 2 * width:3 * width])
    g = _dot(h, w_ref[:, 3 * width:4 * width])
    if rope:
        cos = cos_ref[...]
        sin = sin_ref[...]
        q = _rope_tile(q, cos, sin)
        k = _rope_tile(k, cos, sin)
    q_ref[...] = (q * q_scale).astype(BF16)
    kf_ref[...] = k
    kb_ref[...] = k.astype(BF16)
    vf_ref[...] = v
    vb_ref[...] = v.astype(BF16)
    g_ref[...] = (g * jax.nn.sigmoid(g)).astype(BF16)


def _project(y, norm_w, w_in_bf, cos_t, sin_t, *, rope, q_scale, tm):
    rows, d = y.shape
    width = w_in_bf.shape[1] // 4
    n_pos_tiles = cos_t.shape[0] // tm
    row_spec = lambda w: pl.BlockSpec((tm, w), lambda i: (i, 0))
    tab_spec = pl.BlockSpec((tm, DIFF_DH), lambda i: (i % n_pos_tiles, 0))
    bf = jax.ShapeDtypeStruct((rows, width), BF16)
    f32 = jax.ShapeDtypeStruct((rows, width), F32)
    return pl.pallas_call(
        functools.partial(_proj_kernel, rope=rope, q_scale=q_scale, width=width),
        grid=(rows // tm,),
        in_specs=[row_spec(d),
                  pl.BlockSpec((1, d), lambda i: (0, 0)),
                  pl.BlockSpec((d, 4 * width), lambda i: (0, 0)),
                  tab_spec, tab_spec],
        out_specs=[row_spec(width)] * 6,
        out_shape=[bf, f32, f32, bf, bf, bf],
        compiler_params=pltpu.CompilerParams(
            dimension_semantics=("arbitrary",), vmem_limit_bytes=V7X_VMEM_LIMIT),
        name="in_proj",
    )(y, norm_w, w_in_bf, cos_t, sin_t)


def _out_kernel(o_ref, w_ref, y_ref, fw_ref, out_ref, *, final):
    y = y_ref[...] + _dot(o_ref[...], w_ref[...])
    if final:
        y = _rms(y, fw_ref[...])
    out_ref[...] = y


def _out_project(o_bf, w_out_bf, y, final_w, *, final, tm):
    rows, d = y.shape
    width = o_bf.shape[1]
    return pl.pallas_call(
        functools.partial(_out_kernel, final=final),
        grid=(rows // tm,),
        in_specs=[pl.BlockSpec((tm, width), lambda i: (i, 0)),
                  pl.BlockSpec((width, d), lambda i: (0, 0)),
                  pl.BlockSpec((tm, d), lambda i: (i, 0)),
                  pl.BlockSpec((1, d), lambda i: (0, 0))],
        out_specs=pl.BlockSpec((tm, d), lambda i: (i, 0)),
        out_shape=jax.ShapeDtypeStruct((rows, d), F32),
        compiler_params=pltpu.CompilerParams(
            dimension_semantics=("arbitrary",), vmem_limit_bytes=V7X_VMEM_LIMIT),
        name="out_proj",
    )(o_bf, w_out_bf, y, final_w)


def _diff_lambda(lam_ref, lam_init):
    lf = lam_ref[...]
    a = jnp.sum(lf[0:1] * lf[1:2], axis=-1, keepdims=True)
    b = jnp.sum(lf[2:3] * lf[3:4], axis=-1, keepdims=True)
    return jnp.exp(a) - jnp.exp(b) + lam_init


def _diff_finish(acc1, l1, acc2, l2, lam, sw, gate, lam_init):
    o = acc1 / l1 - lam * (acc2 / l2)
    o = _rms(o, sw) * (1.0 - lam_init)
    return (o * gate).astype(BF16)


def _softplus(z):
    return jnp.maximum(z, 0.0) + jnp.log(1.0 + jnp.exp(-jnp.abs(z)))


def _suffix_ones(n):
    j = lax.broadcasted_iota(jnp.int32, (n, n), 0)
    s = lax.broadcasted_iota(jnp.int32, (n, n), 1)
    return (j >= s).astype(BF16)


def _diff_prompt_kernel(q_ref, k_ref, v_ref, g_ref, lam_ref, sw_ref, o_ref,
                        m1_ref, l1_ref, a1_ref, m2_ref, l2_ref, a2_ref, *, tq, tk, lam_init):
    qi = pl.program_id(2)
    q = q_ref[0]
    q1 = q[:, :DIFF_DH]
    q2 = q[:, DIFF_DH:]
    for m_ref, l_ref, a_ref in ((m1_ref, l1_ref, a1_ref), (m2_ref, l2_ref, a2_ref)):
        m_ref[...] = jnp.full(m_ref.shape, NEG_INF, F32)
        l_ref[...] = jnp.zeros(l_ref.shape, F32)
        a_ref[...] = jnp.zeros(a_ref.shape, F32)

    def one_map(qm, km, vs, visible, m_ref, l_ref, a_ref):
        s = _nt_dot(qm, km)
        if visible is not None:
            s = jnp.where(visible, s, NEG_INF)
        m_old = m_ref[...]
        m_new = jnp.maximum(m_old, jnp.max(s, axis=-1, keepdims=True))
        p = jnp.exp(s - m_new)
        alpha = jnp.exp(m_old - m_new)
        l_ref[...] = alpha * l_ref[...] + jnp.sum(p, axis=-1, keepdims=True)
        a_ref[...] = alpha * a_ref[...] + _dot(p.astype(BF16), vs)
        m_ref[...] = m_new

    def step(kb, visible):
        start = pl.multiple_of(kb * tk, tk)
        ks = k_ref[0, pl.ds(start, tk), :]
        vs = v_ref[0, pl.ds(start, tk), :]
        one_map(q1, ks[:, :DIFF_DH], vs, visible, m1_ref, l1_ref, a1_ref)
        one_map(q2, ks[:, DIFF_DH:], vs, visible, m2_ref, l2_ref, a2_ref)

    per_q = tq // tk
    n_full = qi * per_q

    def body(kb, carry):
        step(kb, None)
        return carry

    lax.fori_loop(0, n_full, body, 0)
    row = lax.broadcasted_iota(jnp.int32, (tq, tk), 0)
    col = lax.broadcasted_iota(jnp.int32, (tq, tk), 1)
    for d in range(per_q):
        visible = ((col + d * tk) // CHUNK) <= (row // CHUNK)
        step(n_full + d, visible)

    lam = _diff_lambda(lam_ref, lam_init)
    o_ref[0] = _diff_finish(a1_ref[...], l1_ref[...], a2_ref[...], l2_ref[...], lam,
                            sw_ref[...], g_ref[0].astype(F32), lam_init)


def _sb_prompt_kernel(q_ref, k_ref, v_ref, g_ref, o_ref, c_ref, acc_ref, *, tq, tk):
    qi = pl.program_id(2)
    q = q_ref[0]
    tri = _suffix_ones(tk)
    c_ref[...] = jnp.zeros(c_ref.shape, F32)
    acc_ref[...] = jnp.zeros(acc_ref.shape, F32)

    def step(kb, earlier):
        start = pl.multiple_of(kb * tk, tk)
        ks = k_ref[0, pl.ds(start, tk), :]
        vs = v_ref[0, pl.ds(start, tk), :]
        z = _nt_dot(q, ks)
        sp = _softplus(z)
        if earlier is not None:
            sp = jnp.where(earlier, sp, 0.0)
        w = _dot(sp.astype(BF16), tri)
        c = c_ref[...]
        a = jnp.exp(z - w - c)
        if earlier is not None:
            a = jnp.where(earlier, a, 0.0)
        acc_ref[...] += _dot(a.astype(BF16), vs)
        c_ref[...] = c + w[:, 0:1]

    per_q = tq // tk
    n_full = qi * per_q
    row = lax.broadcasted_iota(jnp.int32, (tq, tk), 0)
    col = lax.broadcasted_iota(jnp.int32, (tq, tk), 1)
    for d in reversed(range(per_q)):
        step(n_full + d, (col + d * tk) < row)

    def body(i, carry):
        step(n_full - 1 - i, None)
        return carry

    lax.fori_loop(0, n_full, body, 0)
    o_ref[0] = (acc_ref[...] * g_ref[0].astype(F32)).astype(BF16)


def _prompt_attention(q, k, v, g, lam_vecs, subln_w, *, diff, lam_init, tq):
    b, t, width = q.shape
    heads = width // HEAD_W
    tile = pl.BlockSpec((1, tq, HEAD_W), lambda bi, hi, qi: (bi, qi, hi))
    full = pl.BlockSpec((1, t, HEAD_W), lambda bi, hi, qi: (bi, 0, hi))
    params = pltpu.CompilerParams(dimension_semantics=("arbitrary",) * 3,
                                  vmem_limit_bytes=V7X_VMEM_LIMIT)
    out_shape = jax.ShapeDtypeStruct((b, t, width), BF16)
    grid = (b, heads, t // tq)
    if diff:
        tk = min(DIFF_K_TILE, tq)
        stat = pltpu.VMEM((tq, 1), F32)
        acc = pltpu.VMEM((tq, HEAD_W), F32)
        return pl.pallas_call(
            functools.partial(_diff_prompt_kernel, tq=tq, tk=tk, lam_init=lam_init),
            grid=grid,
            in_specs=[tile, full, full, tile,
                      pl.BlockSpec((4, DIFF_DH), lambda bi, hi, qi: (0, 0)),
                      pl.BlockSpec((1, HEAD_W), lambda bi, hi, qi: (0, 0))],
            out_specs=tile, out_shape=out_shape,
            scratch_shapes=[stat, stat, acc, stat, stat, acc],
            compiler_params=params, name="diff_prompt",
        )(q, k, v, g, lam_vecs, subln_w)
    tk = min(SB_K_TILE, tq)
    return pl.pallas_call(
        functools.partial(_sb_prompt_kernel, tq=tq, tk=tk),
        grid=grid,
        in_specs=[tile, full, full, tile],
        out_specs=tile, out_shape=out_shape,
        scratch_shapes=[pltpu.VMEM((tq, 1), F32), pltpu.VMEM((tq, HEAD_W), F32)],
        compiler_params=params, name="sb_prompt",
    )(q, k, v, g)


def _diff_sample_kernel(q_ref, kn_ref, vn_ref, kc_ref, vc_ref, g_ref, lam_ref, sw_ref, o_ref,
                        *, past, lam_init):
    q = q_ref[0]
    kn = kn_ref[0]
    vn = vn_ref[0]
    kc = kc_ref[0, 0].astype(BF16)
    vc = vc_ref[0, 0].astype(BF16)
    t = q.shape[0]
    row = lax.broadcasted_iota(jnp.int32, (t, t), 0)
    col = lax.broadcasted_iota(jnp.int32, (t, t), 1)
    visible = ((past + col) // CHUNK) <= ((past + row) // CHUNK)

    def one_map(lo):
        qm = q[:, lo:lo + DIFF_DH]
        sp = _nt_dot(qm, kc[:, lo:lo + DIFF_DH])
        sn = jnp.where(visible, _nt_dot(qm, kn[:, lo:lo + DIFF_DH]), NEG_INF)
        m = jnp.maximum(jnp.max(sp, axis=-1, keepdims=True), jnp.max(sn, axis=-1, keepdims=True))
        pp = jnp.exp(sp - m)
        pn = jnp.exp(sn - m)
        l = jnp.sum(pp, axis=-1, keepdims=True) + jnp.sum(pn, axis=-1, keepdims=True)
        return _dot(pp.astype(BF16), vc) + _dot(pn.astype(BF16), vn), l

    acc1, l1 = one_map(0)
    acc2, l2 = one_map(DIFF_DH)
    lam = _diff_lambda(lam_ref, lam_init)
    o_ref[0] = _diff_finish(acc1, l1, acc2, l2, lam, sw_ref[...], g_ref[0].astype(F32), lam_init)


def _sb_sample_kernel(q_ref, kn_ref, vn_ref, kc_ref, vc_ref, g_ref, o_ref, *, past, tk):
    q = q_ref[0]
    t = q.shape[0]
    row = lax.broadcasted_iota(jnp.int32, (t, t), 0)
    col = lax.broadcasted_iota(jnp.int32, (t, t), 1)
    earlier = col < row
    z = _nt_dot(q, kn_ref[0])
    sp = jnp.where(earlier, _softplus(z), 0.0)
    w = _dot(sp.astype(BF16), _suffix_ones(t))
    a = jnp.where(earlier, jnp.exp(z - w), 0.0)
    acc = _dot(a.astype(BF16), vn_ref[0])
    c = w[:, 0:1]
    tri = _suffix_ones(tk)
    for kb in reversed(range(past // tk)):
        ks = kc_ref[0, 0, kb * tk:(kb + 1) * tk, :].astype(BF16)
        vs = vc_ref[0, 0, kb * tk:(kb + 1) * tk, :].astype(BF16)
        z = _nt_dot(q, ks)
        w = _dot(_softplus(z).astype(BF16), tri)
        acc = acc + _dot(jnp.exp(z - w - c).astype(BF16), vs)
        c = c + w[:, 0:1]
    o_ref[0] = (acc * g_ref[0].astype(F32)).astype(BF16)


def _sample_attention(q, kn, vn, cache_k, cache_v, layer, g, lam_vecs, subln_w, *, diff, lam_init):
    b, t, width = q.shape
    past = cache_k.shape[2]
    heads = width // HEAD_W
    tile = pl.BlockSpec((1, t, HEAD_W), lambda bi, hi: (bi, 0, hi))
    cache = pl.BlockSpec((1, 1, past, HEAD_W), lambda bi, hi: (layer, bi, 0, hi))
    params = pltpu.CompilerParams(dimension_semantics=("arbitrary",) * 2,
                                  vmem_limit_bytes=V7X_VMEM_LIMIT)
    out_shape = jax.ShapeDtypeStruct((b, t, width), BF16)
    if diff:
        return pl.pallas_call(
            functools.partial(_diff_sample_kernel, past=past, lam_init=lam_init),
            grid=(b, heads),
            in_specs=[tile, tile, tile, cache, cache, tile,
                      pl.BlockSpec((4, DIFF_DH), lambda bi, hi: (0, 0)),
                      pl.BlockSpec((1, HEAD_W), lambda bi, hi: (0, 0))],
            out_specs=tile, out_shape=out_shape, compiler_params=params, name="diff_sample",
        )(q, kn, vn, cache_k, cache_v, g, lam_vecs, subln_w)
    tk = math.gcd(past, SB_K_TILE)
    return pl.pallas_call(
        functools.partial(_sb_sample_kernel, past=past, tk=tk),
        grid=(b, heads),
        in_specs=[tile, tile, tile, cache, cache, tile],
        out_specs=tile, out_shape=out_shape, compiler_params=params, name="sb_sample",
    )(q, kn, vn, cache_k, cache_v, g)


def _rope_tables(positions):
    half = DIFF_DH // 2
    inv = ROPE_THETA ** (-jnp.arange(half, dtype=F32) / half)
    ang = positions.astype(F32)[:, None] * inv[None, :]
    cos = jnp.cos(ang)
    sin = jnp.sin(ang)
    return jnp.concatenate([cos, cos], axis=1), jnp.concatenate([-sin, sin], axis=1)


def kernel(x_prompt, x_sample, cache_k, cache_v, norm_w, w_in, w_out, diff_lambda, diff_subln_w,
           final_norm_w):
    depth, d_model = norm_w.shape
    b, t, _ = x_prompt.shape
    sb, st, _ = x_sample.shape
    past = cache_k.shape[2]
    width = w_out.shape[1]
    assert width == DIFF_HEADS * 2 * DIFF_DH == SB_HEADS * SB_DH
    assert past % CHUNK == 0 and st % 16 == 0

    tm_p = min(ROW_TILE, t)
    tm_s = min(ROW_TILE, sb * st)
    assert t % tm_p == 0 and (sb * st) % tm_s == 0 and tm_s % st == 0
    tq = min(Q_TILE, t)
    assert t % tq == 0

    cos_p, sin_p = _rope_tables(jnp.arange(t, dtype=jnp.int32))
    cos_s, sin_s = _rope_tables(past + jnp.arange(st, dtype=jnp.int32))
    cos_s = jnp.tile(cos_s, (tm_s // st, 1))
    sin_s = jnp.tile(sin_s, (tm_s // st, 1))

    w_in_bf = w_in.astype(BF16)
    w_out_bf = w_out.astype(BF16)
    final_w = final_norm_w.reshape(1, d_model)

    yp = x_prompt.reshape(b * t, d_model)
    ys = x_sample.reshape(sb * st, d_model)
    kp_list, vp_list, ks_list, vs_list = [], [], [], []
    for i in range(depth):
        diff = i % 2 == 0
        j = i // 2
        lam_init = 0.8 - 0.6 * math.exp(-0.3 * i)
        q_scale = (DIFF_DH if diff else SB_DH) ** -0.5
        nw = norm_w[i].reshape(1, d_model)
        lam_vecs = diff_lambda[j]
        subln = diff_subln_w[j].reshape(1, HEAD_W)
        final = i == depth - 1

        q, kf, vf, kb, vb, g = _project(yp, nw, w_in_bf[i], cos_p, sin_p,
                                        rope=diff, q_scale=q_scale, tm=tm_p)
        shp = (b, t, width)
        o = _prompt_attention(q.reshape(shp), kb.reshape(shp), vb.reshape(shp), g.reshape(shp),
                              lam_vecs, subln, diff=diff, lam_init=lam_init, tq=tq)
        yp = _out_project(o.reshape(b * t, width), w_out_bf[i], yp, final_w, final=final, tm=tm_p)
        kp_list.append(kf.reshape(shp))
        vp_list.append(vf.reshape(shp))

        q, kf, vf, kb, vb, g = _project(ys, nw, w_in_bf[i], cos_s, sin_s,
                                        rope=diff, q_scale=q_scale, tm=tm_s)
        shp = (sb, st, width)
        o = _sample_attention(q.reshape(shp), kb.reshape(shp), vb.reshape(shp), cache_k, cache_v, i,
                              g.reshape(shp), lam_vecs, subln, diff=diff, lam_init=lam_init)
        ys = _out_project(o.reshape(sb * st, width), w_out_bf[i], ys, final_w, final=final, tm=tm_s)
        ks_list.append(kf.reshape(shp))
        vs_list.append(vf.reshape(shp))

    return (yp.reshape(b, t, d_model), ys.reshape(sb, st, d_model),
            jnp.stack(kp_list, axis=0), jnp.stack(vp_list, axis=0),
            jnp.stack(ks_list, axis=0), jnp.stack(vs_list, axis=0))
```

```python
import functools
import math

import jax
import jax.numpy as jnp
from jax import lax
from jax.experimental import pallas as pl
from jax.experimental.pallas import tpu as pltpu

F32 = jnp.float32
BF16 = jnp.bfloat16

CHUNK = 64
DIFF_HEADS = 4
DIFF_DH = 128
SB_HEADS = 4
SB_DH = 256
HEAD_W = 256
ROPE_THETA = 10000.0
EPS = 1e-6
NEG_INF = -1e30
LOG2E = math.log2(math.e)

V7X_VMEM_LIMIT = 56 * 1024 * 1024
V7X_MXU_DEPTH = 256
ROW_TILE = 512
ATTN_TILE = 512
SB_SUB = V7X_MXU_DEPTH
LOOP_UNROLL = 2


def _unrolled_loop(n, body, unroll):
    main = lax.shift_right_logical(n, int(math.log2(unroll)))

    def group(j, carry):
        for u in range(unroll):
            carry = body(j * unroll + u, carry)
        return carry

    lax.fori_loop(0, main, group, 0)
    lax.fori_loop(main * unroll, n, body, 0)


def _nt_dot(a, b):
    return lax.dot_general(a, b, (((1,), (1,)), ((), ())), preferred_element_type=F32)


def _dot(a, b):
    return jnp.dot(a, b, preferred_element_type=F32)


def _rms(x, w):
    return x * lax.rsqrt(jnp.mean(x * x, axis=-1, keepdims=True) + EPS) * w


def _rope_tile(x, cos, sin_signed):
    outs = []
    for g in range(x.shape[1] // DIFF_DH):
        xg = x[:, g * DIFF_DH:(g + 1) * DIFF_DH]
        outs.append(xg * cos + pltpu.roll(xg, DIFF_DH // 2, axis=1) * sin_signed)
    return jnp.concatenate(outs, axis=1)


def _proj_kernel(y_ref, nw_ref, w_ref, wvt_ref, cos_ref, sin_ref,
                 q_ref, kf_ref, vf_ref, kb_ref, vb_ref, g_ref, *, rope, q_scale, width, v_transposed):
    h = _rms(y_ref[...], nw_ref[...]).astype(BF16)
    q = _dot(h, w_ref[:, 0 * width:1 * width])
    k = _dot(h, w_ref[:, 1 * width:2 * width])
    v = _dot(h, w_ref[:, 2 * width:3 * width])
    g = _dot(h, w_ref[:, 3 * width:4 * width])
    if rope:
        cos = cos_ref[...]
        sin = sin_ref[...]
        q = _rope_tile(q, cos, sin)
        k = _rope_tile(k, cos, sin)
    q_ref[...] = (q * q_scale).astype(BF16)
    kf_ref[...] = k
    kb_ref[...] = k.astype(BF16)
    vf_ref[...] = v
    if v_transposed:
        vb_ref[0] = _nt_dot(wvt_ref[...], h).astype(BF16)
    else:
        vb_ref[...] = v.astype(BF16)
    g_ref[...] = (g * jax.nn.sigmoid(g)).astype(BF16)


def _project(y, norm_w, w_in_bf, w_vt_bf, cos_t, sin_t, *, rope, q_scale, tm, v_transposed):
    rows, d = y.shape
    width = w_in_bf.shape[1] // 4
    n_pos_tiles = cos_t.shape[0] // tm
    row_spec = lambda w: pl.BlockSpec((tm, w), lambda i: (i, 0))
    tab_spec = pl.BlockSpec((tm, DIFF_DH), lambda i: (i % n_pos_tiles, 0))
    bf = jax.ShapeDtypeStruct((rows, width), BF16)
    f32 = jax.ShapeDtypeStruct((rows, width), F32)
    if v_transposed:
        vb_shape = jax.ShapeDtypeStruct((rows // tm, width, tm), BF16)
        vb_spec = pl.BlockSpec((1, width, tm), lambda i: (i, 0, 0))
    else:
        vb_shape, vb_spec = bf, row_spec(width)
    resident = functools.partial(pl.BlockSpec, pipeline_mode=pl.Buffered(1))
    return pl.pallas_call(
        functools.partial(_proj_kernel, rope=rope, q_scale=q_scale, width=width,
                          v_transposed=v_transposed),
        grid=(rows // tm,),
        in_specs=[row_spec(d),
                  pl.BlockSpec((1, d), lambda i: (0, 0)),
                  resident((d, 4 * width), lambda i: (0, 0)),
                  resident((width, d), lambda i: (0, 0)),
                  tab_spec, tab_spec],
        out_specs=[row_spec(width)] * 4 + [vb_spec, row_spec(width)],
        out_shape=[bf, f32, f32, bf, vb_shape, bf],
        compiler_params=pltpu.CompilerParams(
            dimension_semantics=("arbitrary",), vmem_limit_bytes=V7X_VMEM_LIMIT),
        name="in_proj",
    )(y, norm_w, w_in_bf, w_vt_bf, cos_t, sin_t)


def _out_kernel(o_ref, w_ref, y_ref, fw_ref, out_ref, *, final):
    y = y_ref[...] + _dot(o_ref[...], w_ref[...])
    if final:
        y = _rms(y, fw_ref[...])
    out_ref[...] = y


def _out_project(o_bf, w_out_bf, y, final_w, *, final, tm):
    rows, d = y.shape
    width = o_bf.shape[1]
    return pl.pallas_call(
        functools.partial(_out_kernel, final=final),
        grid=(rows // tm,),
        in_specs=[pl.BlockSpec((tm, width), lambda i: (i, 0)),
                  pl.BlockSpec((width, d), lambda i: (0, 0)),
                  pl.BlockSpec((tm, d), lambda i: (i, 0)),
                  pl.BlockSpec((1, d), lambda i: (0, 0))],
        out_specs=pl.BlockSpec((tm, d), lambda i: (i, 0)),
        out_shape=jax.ShapeDtypeStruct((rows, d), F32),
        compiler_params=pltpu.CompilerParams(
            dimension_semantics=("arbitrary",), vmem_limit_bytes=V7X_VMEM_LIMIT),
        name="out_proj",
    )(o_bf, w_out_bf, y, final_w)


def _diff_lambda(lam_ref, lam_init):
    lf = lam_ref[...]
    a = jnp.sum(lf[0:1] * lf[1:2], axis=-1, keepdims=True)
    b = jnp.sum(lf[2:3] * lf[3:4], axis=-1, keepdims=True)
    return jnp.exp(a) - jnp.exp(b) + lam_init


def _diff_finish(o, sw, gate, lam_init):
    o = _rms(o, sw) * (1.0 - lam_init)
    return (o * gate).astype(BF16)


def _break_logs(z):
    lg = jnp.log(1.0 + jnp.exp(-jnp.abs(z)))
    return jnp.maximum(z, 0.0) + lg, jnp.minimum(z, 0.0) - lg


def _later_ones(n):
    j = lax.broadcasted_iota(jnp.int32, (n, n), 0)
    s = lax.broadcasted_iota(jnp.int32, (n, n), 1)
    return (j > s).astype(BF16)


def _diff_prompt_kernel(q_ref, k_ref, vt_ref, g_ref, lam_ref, sw_ref, o_ref,
                        s_ref, p_ref, al_ref, m_ref, l_ref, acc_ref, *, tile, lam_init):
    qi = pl.program_id(2)
    q = q_ref[0]
    qm = (q[:, :DIFF_DH], q[:, DIFF_DH:])
    m_ref[...] = jnp.full(m_ref.shape, NEG_INF, F32)
    l_ref[...] = jnp.zeros(l_ref.shape, F32)
    acc_ref[...] = jnp.zeros(acc_ref.shape, F32)

    def scores(kb):
        start = pl.multiple_of(kb * tile, tile)
        ks = k_ref[0, pl.ds(start, tile), :]
        for i in range(2):
            s_ref[i] = _nt_dot(ks[:, i * DIFF_DH:(i + 1) * DIFF_DH], qm[i])

    def softmax(visible):
        for i in range(2):
            s = s_ref[i]
            if visible is not None:
                s = jnp.where(visible, s, NEG_INF)
            m_old = m_ref[i]
            m_new = jnp.maximum(m_old, jnp.max(s, axis=0, keepdims=True))
            p = jnp.exp2(s - m_new)
            alpha = jnp.exp2(m_old - m_new)
            l_ref[i] = alpha * l_ref[i] + jnp.sum(p, axis=0, keepdims=True)
            p_ref[i] = p.astype(BF16)
            al_ref[i] = alpha
            m_ref[i] = m_new

    def values(kb):
        vt = vt_ref[kb]
        for i in range(2):
            acc_ref[i] = al_ref[i] * acc_ref[i] + _dot(vt, p_ref[i])

    key = lax.broadcasted_iota(jnp.int32, (tile, tile), 0)
    qry = lax.broadcasted_iota(jnp.int32, (tile, tile), 1)
    scores(qi)
    softmax((key // CHUNK) <= (qry // CHUNK))
    scores(0)

    def body(i, carry):
        values(jnp.where(i == 0, qi, i - 1))
        softmax(None)
        scores(jnp.minimum(i + 1, qi))
        return carry

    _unrolled_loop(qi, body, LOOP_UNROLL)
    values(jnp.maximum(qi - 1, 0))

    lam = _diff_lambda(lam_ref, lam_init)
    ot = acc_ref[0] / l_ref[0] - lam * (acc_ref[1] / l_ref[1])
    o_ref[0] = _diff_finish(ot.T, sw_ref[...], g_ref[0].astype(F32), lam_init)


def _sb_prompt_kernel(q_ref, k_ref, vt_ref, g_ref, o_ref, z_ref, a_ref, c_ref, acc_ref, *, tile, sub):
    qi = pl.program_id(2)
    q = q_ref[0]
    j = lax.broadcasted_iota(jnp.int32, (sub, sub), 0)
    s = lax.broadcasted_iota(jnp.int32, (sub, sub), 1)
    upper = (s > j).astype(BF16)
    c_ref[...] = jnp.zeros(c_ref.shape, F32)
    acc_ref[...] = jnp.zeros(acc_ref.shape, F32)

    def logits(kb):
        start = pl.multiple_of(kb * tile, tile)
        z_ref[...] = _nt_dot(k_ref[0, pl.ds(start, tile), :], q)

    def weights(diagonal):
        c = c_ref[...]
        for d in reversed(range(tile // sub)):
            z = z_ref[d * sub:(d + 1) * sub, :]
            lg = jnp.log2(1.0 + jnp.exp2(-jnp.abs(z)))
            sp = jnp.maximum(z, 0.0) + lg
            ls = jnp.minimum(z, 0.0) - lg
            if diagonal:
                key = lax.broadcasted_iota(jnp.int32, (sub, tile), 0) + d * sub
                qry = lax.broadcasted_iota(jnp.int32, (sub, tile), 1)
                earlier = key < qry
                sp = jnp.where(earlier, sp, 0.0)
            sp = sp.astype(BF16)
            w = _dot(upper, sp)
            a = jnp.exp2(ls - w - c)
            if diagonal:
                a = jnp.where(earlier, a, 0.0)
            a_ref[d * sub:(d + 1) * sub, :] = a.astype(BF16)
            c = c + w[0:1, :] + sp[0:1, :].astype(F32)
        c_ref[...] = c

    def values(kb):
        acc_ref[...] += _dot(vt_ref[kb], a_ref[...])

    logits(qi)
    weights(True)
    logits(jnp.maximum(qi - 1, 0))

    def body(i, carry):
        kb = qi - 1 - i
        values(kb + 1)
        weights(False)
        logits(jnp.maximum(kb - 1, 0))
        return carry

    _unrolled_loop(qi, body, LOOP_UNROLL)
    values(0)
    o_ref[0] = (acc_ref[...].T * g_ref[0].astype(F32)).astype(BF16)


def _prompt_attention(q, k, vt, g, lam_vecs, subln_w, *, diff, lam_init, tile):
    b, t, width = q.shape
    heads = width // HEAD_W
    nk = t // tile
    vt = vt.reshape(b * nk, width, tile)
    q_tile = pl.BlockSpec((1, tile, HEAD_W), lambda bi, hi, qi: (bi, qi, hi))
    k_full = pl.BlockSpec((1, t, HEAD_W), lambda bi, hi, qi: (bi, 0, hi))
    vt_full = pl.BlockSpec((nk, HEAD_W, tile), lambda bi, hi, qi: (bi, hi, 0))
    params = pltpu.CompilerParams(dimension_semantics=("arbitrary",) * 3,
                                  vmem_limit_bytes=V7X_VMEM_LIMIT)
    out_shape = jax.ShapeDtypeStruct((b, t, width), BF16)
    grid = (b, heads, nk)
    if diff:
        stat = pltpu.VMEM((2, 1, tile), F32)
        return pl.pallas_call(
            functools.partial(_diff_prompt_kernel, tile=tile, lam_init=lam_init),
            grid=grid,
            in_specs=[q_tile, k_full, vt_full, q_tile,
                      pl.BlockSpec((4, DIFF_DH), lambda bi, hi, qi: (0, 0)),
                      pl.BlockSpec((1, HEAD_W), lambda bi, hi, qi: (0, 0))],
            out_specs=q_tile, out_shape=out_shape,
            scratch_shapes=[pltpu.VMEM((2, tile, tile), F32), pltpu.VMEM((2, tile, tile), BF16),
                            stat, stat, stat, pltpu.VMEM((2, HEAD_W, tile), F32)],
            compiler_params=params, name="diff_prompt",
        )(q, k, vt, g, lam_vecs, subln_w)
    sub = math.gcd(SB_SUB, tile)
    return pl.pallas_call(
        functools.partial(_sb_prompt_kernel, tile=tile, sub=sub),
        grid=grid,
        in_specs=[q_tile, k_full, vt_full, q_tile],
        out_specs=q_tile, out_shape=out_shape,
        scratch_shapes=[pltpu.VMEM((tile, tile), F32), pltpu.VMEM((tile, tile), BF16),
                        pltpu.VMEM((1, tile), F32), pltpu.VMEM((HEAD_W, tile), F32)],
        compiler_params=params, name="sb_prompt",
    )(q, k, vt, g)


def _diff_sample_kernel(q_ref, kn_ref, vn_ref, kc_ref, vc_ref, g_ref, lam_ref, sw_ref, o_ref,
                        *, past, lam_init):
    q = q_ref[0]
    kn = kn_ref[0]
    vn = vn_ref[0]
    kc = kc_ref[0, 0].astype(BF16)
    vc = vc_ref[0, 0].astype(BF16)
    t = q.shape[0]
    row = lax.broadcasted_iota(jnp.int32, (t, t), 0)
    col = lax.broadcasted_iota(jnp.int32, (t, t), 1)
    visible = ((past + col) // CHUNK) <= ((past + row) // CHUNK)

    def one_map(lo):
        qm = q[:, lo:lo + DIFF_DH]
        sp = _nt_dot(qm, kc[:, lo:lo + DIFF_DH])
        sn = jnp.where(visible, _nt_dot(qm, kn[:, lo:lo + DIFF_DH]), NEG_INF)
        m = jnp.maximum(jnp.max(sp, axis=-1, keepdims=True), jnp.max(sn, axis=-1, keepdims=True))
        pp = jnp.exp(sp - m)
        pn = jnp.exp(sn - m)
        l = jnp.sum(pp, axis=-1, keepdims=True) + jnp.sum(pn, axis=-1, keepdims=True)
        return (_dot(pp.astype(BF16), vc) + _dot(pn.astype(BF16), vn)) / l

    lam = _diff_lambda(lam_ref, lam_init)
    o = one_map(0) - lam * one_map(DIFF_DH)
    o_ref[0] = _diff_finish(o, sw_ref[...], g_ref[0].astype(F32), lam_init)


def _sb_sample_kernel(q_ref, kn_ref, vn_ref, kc_ref, vc_ref, g_ref, o_ref, *, past, tk):
    q = q_ref[0]
    t = q.shape[0]
    row = lax.broadcasted_iota(jnp.int32, (t, t), 0)
    col = lax.broadcasted_iota(jnp.int32, (t, t), 1)
    earlier = col < row
    z = _nt_dot(q, kn_ref[0])
    sp, ls = _break_logs(z)
    sp = jnp.where(earlier, sp, 0.0).astype(BF16)
    w = _dot(sp, _later_ones(t))
    a = jnp.where(earlier, jnp.exp(ls - w), 0.0)
    acc = _dot(a.astype(BF16), vn_ref[0])
    c = w[:, 0:1] + sp[:, 0:1].astype(F32)
    tri = _later_ones(tk)
    for kb in reversed(range(past // tk)):
        ks = kc_ref[0, 0, kb * tk:(kb + 1) * tk, :].astype(BF16)
        vs = vc_ref[0, 0, kb * tk:(kb + 1) * tk, :].astype(BF16)
        sp, ls = _break_logs(_nt_dot(q, ks))
        sp = sp.astype(BF16)
        w = _dot(sp, tri)
        acc = acc + _dot(jnp.exp(ls - w - c).astype(BF16), vs)
        c = c + w[:, 0:1] + sp[:, 0:1].astype(F32)
    o_ref[0] = (acc * g_ref[0].astype(F32)).astype(BF16)


def _sample_attention(q, kn, vn, cache_k, cache_v, layer, g, lam_vecs, subln_w, *, diff, lam_init):
    b, t, width = q.shape
    past = cache_k.shape[2]
    heads = width // HEAD_W
    tile = pl.BlockSpec((1, t, HEAD_W), lambda bi, hi: (bi, 0, hi))
    cache = pl.BlockSpec((1, 1, past, HEAD_W), lambda bi, hi: (layer, bi, 0, hi))
    params = pltpu.CompilerParams(dimension_semantics=("arbitrary",) * 2,
                                  vmem_limit_bytes=V7X_VMEM_LIMIT)
    out_shape = jax.ShapeDtypeStruct((b, t, width), BF16)
    if diff:
        return pl.pallas_call(
            functools.partial(_diff_sample_kernel, past=past, lam_init=lam_init),
            grid=(b, heads),
            in_specs=[tile, tile, tile, cache, cache, tile,
                      pl.BlockSpec((4, DIFF_DH), lambda bi, hi: (0, 0)),
                      pl.BlockSpec((1, HEAD_W), lambda bi, hi: (0, 0))],
            out_specs=tile, out_shape=out_shape, compiler_params=params, name="diff_sample",
        )(q, kn, vn, cache_k, cache_v, g, lam_vecs, subln_w)
    tk = math.gcd(past, V7X_MXU_DEPTH)
    return pl.pallas_call(
        functools.partial(_sb_sample_kernel, past=past, tk=tk),
        grid=(b, heads),
        in_specs=[tile, tile, tile, cache, cache, tile],
        out_specs=tile, out_shape=out_shape, compiler_params=params, name="sb_sample",
    )(q, kn, vn, cache_k, cache_v, g)


def _rope_tables(positions):
    half = DIFF_DH // 2
    inv = ROPE_THETA ** (-jnp.arange(half, dtype=F32) / half)
    ang = positions.astype(F32)[:, None] * inv[None, :]
    cos = jnp.cos(ang)
    sin = jnp.sin(ang)
    return jnp.concatenate([cos, cos], axis=1), jnp.concatenate([-sin, sin], axis=1)


def kernel(x_prompt, x_sample, cache_k, cache_v, norm_w, w_in, w_out, diff_lambda, diff_subln_w,
           final_norm_w):
    depth, d_model = norm_w.shape
    b, t, _ = x_prompt.shape
    sb, st, _ = x_sample.shape
    past = cache_k.shape[2]
    width = w_out.shape[1]
    assert width == DIFF_HEADS * 2 * DIFF_DH == SB_HEADS * SB_DH
    assert past % CHUNK == 0 and st % 16 == 0

    tile = min(ATTN_TILE, t)
    tm_p = tile
    tm_s = min(ROW_TILE, sb * st)
    assert t % tile == 0 and tile % CHUNK == 0 and (sb * st) % tm_s == 0 and tm_s % st == 0

    cos_p, sin_p = _rope_tables(jnp.arange(t, dtype=jnp.int32))
    cos_s, sin_s = _rope_tables(past + jnp.arange(st, dtype=jnp.int32))
    cos_s = jnp.tile(cos_s, (tm_s // st, 1))
    sin_s = jnp.tile(sin_s, (tm_s // st, 1))

    w_in_bf = w_in.astype(BF16)
    w_vt_bf = jnp.swapaxes(w_in_bf[:, :, 2 * width:3 * width], 1, 2)
    w_out_bf = w_out.astype(BF16)
    final_w = final_norm_w.reshape(1, d_model)

    yp = x_prompt.reshape(b * t, d_model)
    ys = x_sample.reshape(sb * st, d_model)
    kp_list, vp_list, ks_list, vs_list = [], [], [], []
    for i in range(depth):
        diff = i % 2 == 0
        j = i // 2
        lam_init = 0.8 - 0.6 * math.exp(-0.3 * i)
        q_scale = (DIFF_DH if diff else SB_DH) ** -0.5
        nw = norm_w[i].reshape(1, d_model)
        lam_vecs = diff_lambda[j]
        subln = diff_subln_w[j].reshape(1, HEAD_W)
        final = i == depth - 1

        q, kf, vf, kb, vt, g = _project(yp, nw, w_in_bf[i], w_vt_bf[i], cos_p, sin_p, rope=diff,
                                        q_scale=q_scale * LOG2E, tm=tm_p, v_transposed=True)
        shp = (b, t, width)
        o = _prompt_attention(q.reshape(shp), kb.reshape(shp), vt, g.reshape(shp),
                              lam_vecs, subln, diff=diff, lam_init=lam_init, tile=tile)
        yp = _out_project(o.reshape(b * t, width), w_out_bf[i], yp, final_w, final=final, tm=tm_p)
        kp_list.append(kf.reshape(shp))
        vp_list.append(vf.reshape(shp))

        q, kf, vf, kb, vb, g = _project(ys, nw, w_in_bf[i], w_vt_bf[i], cos_s, sin_s, rope=diff,
                                        q_scale=q_scale, tm=tm_s, v_transposed=False)
        shp = (sb, st, width)
        o = _sample_attention(q.reshape(shp), kb.reshape(shp), vb.reshape(shp), cache_k, cache_v, i,
                              g.reshape(shp), lam_vecs, subln, diff=diff, lam_init=lam_init)
        ys = _out_project(o.reshape(sb * st, width), w_out_bf[i], ys, final_w, final=final, tm=tm_s)
        ks_list.append(kf.reshape(shp))
        vs_list.append(vf.reshape(shp))

    return (yp.reshape(b, t, d_model), ys.reshape(sb, st, d_model),
            jnp.stack(kp_list, axis=0), jnp.stack(vp_list, axis=0),
            jnp.stack(ks_list, axis=0), jnp.stack(vs_list, axis=0))
```

```python
import functools
import math

import jax
import jax.numpy as jnp
from jax import lax
from jax.experimental import pallas as pl
from jax.experimental.pallas import tpu as pltpu

F32 = jnp.float32
BF16 = jnp.bfloat16

CHUNK = 64
DIFF_HEADS = 4
DIFF_DH = 128
SB_HEADS = 4
SB_DH = 256
HEAD_W = 256
ROPE_THETA = 10000.0
EPS = 1e-6
NEG_INF = -1e30
LOG2E = math.log2(math.e)

V7X_VMEM_LIMIT = 56 * 1024 * 1024
V7X_MXU_DEPTH = 256
ROW_TILE = 512
ATTN_TILE = 512
SB_SUB = V7X_MXU_DEPTH
DIFF_UNROLL = 2
SB_UNROLL = 4


def _unrolled_loop(n, body, unroll):
    main = lax.shift_right_logical(n, int(math.log2(unroll)))

    def group(j, carry):
        for u in range(unroll):
            carry = body(j * unroll + u, carry)
        return carry

    lax.fori_loop(0, main, group, 0)
    lax.fori_loop(main * unroll, n, body, 0)


def _nt_dot(a, b):
    return lax.dot_general(a, b, (((1,), (1,)), ((), ())), preferred_element_type=F32)


def _dot(a, b):
    return jnp.dot(a, b, preferred_element_type=F32)


def _rms(x, w):
    return x * lax.rsqrt(jnp.mean(x * x, axis=-1, keepdims=True) + EPS) * w


def _rope_tile(x, cos, sin_signed):
    outs = []
    for g in range(x.shape[1] // DIFF_DH):
        xg = x[:, g * DIFF_DH:(g + 1) * DIFF_DH]
        outs.append(xg * cos + pltpu.roll(xg, DIFF_DH // 2, axis=1) * sin_signed)
    return jnp.concatenate(outs, axis=1)


def _proj_kernel(y_ref, nw_ref, w_ref, cos_ref, sin_ref, *refs,
                 rope, q_scale, width, v_transposed):
    q_ref, kf_ref, vf_ref, kb_ref, vb_ref, g_ref = refs[-6:]
    h = _rms(y_ref[...], nw_ref[...]).astype(BF16)
    q = _dot(h, w_ref[:, 0 * width:1 * width])
    k = _dot(h, w_ref[:, 1 * width:2 * width])
    v = _dot(h, w_ref[:, 2 * width:3 * width])
    g = _dot(h, w_ref[:, 3 * width:4 * width])
    if rope:
        cos = cos_ref[...]
        sin = sin_ref[...]
        q = _rope_tile(q, cos, sin)
        k = _rope_tile(k, cos, sin)
    q_ref[...] = (q * q_scale).astype(BF16)
    kf_ref[...] = k
    kb_ref[...] = k.astype(BF16)
    vf_ref[...] = v
    if v_transposed:
        vb_ref[0] = v.T.astype(BF16)
    else:
        vb_ref[...] = v.astype(BF16)
    g_ref[...] = (g * jax.nn.sigmoid(g)).astype(BF16)


def _project(y, norm_w, w_in_bf, cos_t, sin_t, stacks, layer, depth, *,
             rope, q_scale, tm, v_transposed):
    rows, d = y.shape
    width = w_in_bf.shape[1] // 4
    n_pos_tiles = cos_t.shape[0] // tm
    row_spec = lambda w: pl.BlockSpec((tm, w), lambda i: (i, 0))
    tab_spec = pl.BlockSpec((tm, DIFF_DH), lambda i: (i % n_pos_tiles, 0))
    bf = jax.ShapeDtypeStruct((rows, width), BF16)
    stack_shape = jax.ShapeDtypeStruct((depth, rows, width), F32)
    stack_spec = pl.BlockSpec((None, tm, width), lambda i: (layer, i, 0))
    if v_transposed:
        vb_shape = jax.ShapeDtypeStruct((rows // tm, width, tm), BF16)
        vb_spec = pl.BlockSpec((1, width, tm), lambda i: (i, 0, 0))
    else:
        vb_shape, vb_spec = bf, row_spec(width)
    resident = functools.partial(pl.BlockSpec, pipeline_mode=pl.Buffered(1))
    in_specs = [row_spec(d),
                pl.BlockSpec((1, d), lambda i: (0, 0)),
                resident((d, 4 * width), lambda i: (0, 0)),
                tab_spec, tab_spec]
    args = [y, norm_w, w_in_bf, cos_t, sin_t]
    aliases = {}
    if stacks is not None:
        aliases = {len(args): 1, len(args) + 1: 2}
        in_specs += [pl.BlockSpec(memory_space=pl.ANY)] * 2
        args += list(stacks)
    return pl.pallas_call(
        functools.partial(_proj_kernel, rope=rope, q_scale=q_scale, width=width,
                          v_transposed=v_transposed),
        grid=(rows // tm,),
        in_specs=in_specs,
        out_specs=[row_spec(width), stack_spec, stack_spec, row_spec(width), vb_spec, row_spec(width)],
        out_shape=[bf, stack_shape, stack_shape, bf, vb_shape, bf],
        input_output_aliases=aliases,
        compiler_params=pltpu.CompilerParams(
            dimension_semantics=("arbitrary",), vmem_limit_bytes=V7X_VMEM_LIMIT),
        name="in_proj",
    )(*args)


def _out_kernel(o_ref, w_ref, y_ref, fw_ref, out_ref, *, final):
    y = y_ref[...] + _dot(o_ref[...], w_ref[...])
    if final:
        y = _rms(y, fw_ref[...])
    out_ref[...] = y


def _out_project(o_bf, w_out_bf, y, final_w, *, final, tm):
    rows, d = y.shape
    width = o_bf.shape[1]
    return pl.pallas_call(
        functools.partial(_out_kernel, final=final),
        grid=(rows // tm,),
        in_specs=[pl.BlockSpec((tm, width), lambda i: (i, 0)),
                  pl.BlockSpec((width, d), lambda i: (0, 0)),
                  pl.BlockSpec((tm, d), lambda i: (i, 0)),
                  pl.BlockSpec((1, d), lambda i: (0, 0))],
        out_specs=pl.BlockSpec((tm, d), lambda i: (i, 0)),
        out_shape=jax.ShapeDtypeStruct((rows, d), F32),
        compiler_params=pltpu.CompilerParams(
            dimension_semantics=("arbitrary",), vmem_limit_bytes=V7X_VMEM_LIMIT),
        name="out_proj",
    )(o_bf, w_out_bf, y, final_w)


def _diff_lambda(lam_ref, lam_init):
    lf = lam_ref[...]
    a = jnp.sum(lf[0:1] * lf[1:2], axis=-1, keepdims=True)
    b = jnp.sum(lf[2:3] * lf[3:4], axis=-1, keepdims=True)
    return jnp.exp(a) - jnp.exp(b) + lam_init


def _diff_finish(o, sw, gate, lam_init):
    o = _rms(o, sw) * (1.0 - lam_init)
    return (o * gate).astype(BF16)


def _neg_abs(x):
    bits = lax.bitcast_convert_type(x, jnp.uint32) | jnp.uint32(0x80000000)
    return lax.bitcast_convert_type(bits, F32)


def _break_logs(z):
    lg = jnp.log(1.0 + jnp.exp(-jnp.abs(z)))
    return jnp.maximum(z, 0.0) + lg, jnp.minimum(z, 0.0) - lg


def _later_ones(n):
    j = lax.broadcasted_iota(jnp.int32, (n, n), 0)
    s = lax.broadcasted_iota(jnp.int32, (n, n), 1)
    return (j > s).astype(BF16)


def _diff_prompt_kernel(q_ref, k_ref, vt_ref, g_ref, lam_ref, sw_ref, o_ref,
                        s_ref, p_ref, mx_ref, al_ref, m_ref, l_ref, acc_ref, *, tile, lam_init):
    qi = pl.program_id(2)
    q = q_ref[0]
    qm = (q[:, :DIFF_DH], q[:, DIFF_DH:])
    m_ref[...] = jnp.full(m_ref.shape, NEG_INF, F32)
    l_ref[...] = jnp.zeros(l_ref.shape, F32)
    acc_ref[...] = jnp.zeros(acc_ref.shape, F32)

    def scores(kb):
        start = pl.multiple_of(kb * tile, tile)
        ks = k_ref[0, pl.ds(start, tile), :]
        for i in range(2):
            s = _nt_dot(ks[:, i * DIFF_DH:(i + 1) * DIFF_DH], qm[i])
            s_ref[i] = s
            mx_ref[i] = jnp.max(s, axis=0, keepdims=True)

    def softmax(visible):
        for i in range(2):
            s = s_ref[i]
            if visible is None:
                mx = mx_ref[i]
            else:
                s = jnp.where(visible, s, NEG_INF)
                mx = jnp.max(s, axis=0, keepdims=True)
            m_old = m_ref[i]
            m_new = jnp.maximum(m_old, mx)
            p = jnp.exp2(s - m_new)
            alpha = jnp.exp2(m_old - m_new)
            l_ref[i] = alpha * l_ref[i] + jnp.sum(p, axis=0, keepdims=True)
            p_ref[i] = p.astype(BF16)
            al_ref[i] = alpha
            m_ref[i] = m_new

    def values(kb):
        vt = vt_ref[kb]
        for i in range(2):
            acc_ref[i] = al_ref[i] * acc_ref[i] + _dot(vt, p_ref[i])

    key = lax.broadcasted_iota(jnp.int32, (tile, tile), 0)
    qry = lax.broadcasted_iota(jnp.int32, (tile, tile), 1)
    scores(qi)
    softmax((key // CHUNK) <= (qry // CHUNK))
    scores(0)

    def body(i, carry):
        values(jnp.where(i == 0, qi, i - 1))
        softmax(None)
        scores(jnp.minimum(i + 1, qi))
        return carry

    _unrolled_loop(qi, body, DIFF_UNROLL)
    values(jnp.maximum(qi - 1, 0))

    lam = _diff_lambda(lam_ref, lam_init)
    ot = acc_ref[0] / l_ref[0] - lam * (acc_ref[1] / l_ref[1])
    o_ref[0] = _diff_finish(ot.T, sw_ref[...], g_ref[0].astype(F32), lam_init)


def _sb_prompt_kernel(q_ref, k_ref, vt_ref, g_ref, o_ref, z_ref, a_ref, c_ref, acc_ref, *, tile, sub):
    qi = pl.program_id(2)
    q = q_ref[0]
    j = lax.broadcasted_iota(jnp.int32, (sub, sub), 0)
    s = lax.broadcasted_iota(jnp.int32, (sub, sub), 1)
    upper = (s > j).astype(BF16)
    c_ref[...] = jnp.zeros(c_ref.shape, F32)
    acc_ref[...] = jnp.zeros(acc_ref.shape, F32)

    def logits(kb):
        start = pl.multiple_of(kb * tile, tile)
        z_ref[...] = _nt_dot(k_ref[0, pl.ds(start, tile), :], q)

    def weights(diagonal):
        c = c_ref[...]
        for d in reversed(range(tile // sub)):
            z = z_ref[d * sub:(d + 1) * sub, :]
            sp = jnp.maximum(z, 0.0) + jnp.log2(1.0 + jnp.exp2(_neg_abs(z)))
            ls = z - sp
            if diagonal:
                key = lax.broadcasted_iota(jnp.int32, (sub, tile), 0) + d * sub
                qry = lax.broadcasted_iota(jnp.int32, (sub, tile), 1)
                earlier = key < qry
                sp = jnp.where(earlier, sp, 0.0)
            sp = sp.astype(BF16)
            w = _dot(upper, sp)
            a = jnp.exp2(ls - w - c)
            if diagonal:
                a = jnp.where(earlier, a, 0.0)
            a_ref[d * sub:(d + 1) * sub, :] = a.astype(BF16)
            c = c + w[0:1, :] + sp[0:1, :].astype(F32)
        c_ref[...] = c

    def values(kb):
        acc_ref[...] += _dot(vt_ref[kb], a_ref[...])

    logits(qi)
    weights(True)
    logits(jnp.maximum(qi - 1, 0))

    def body(i, carry):
        kb = qi - 1 - i
        values(kb + 1)
        weights(False)
        logits(jnp.maximum(kb - 1, 0))
        return carry

    _unrolled_loop(qi, body, SB_UNROLL)
    values(0)
    o_ref[0] = (acc_ref[...].T * g_ref[0].astype(F32)).astype(BF16)


def _prompt_attention(q, k, vt, g, lam_vecs, subln_w, *, diff, lam_init, tile):
    b, t, width = q.shape
    heads = width // HEAD_W
    nk = t // tile
    vt = vt.reshape(b * nk, width, tile)
    q_tile = pl.BlockSpec((1, tile, HEAD_W), lambda bi, hi, qi: (bi, qi, hi))
    k_full = pl.BlockSpec((1, t, HEAD_W), lambda bi, hi, qi: (bi, 0, hi))
    vt_full = pl.BlockSpec((nk, HEAD_W, tile), lambda bi, hi, qi: (bi, hi, 0))
    params = pltpu.CompilerParams(dimension_semantics=("arbitrary",) * 3,
                                  vmem_limit_bytes=V7X_VMEM_LIMIT)
    out_shape = jax.ShapeDtypeStruct((b, t, width), BF16)
    grid = (b, heads, nk)
    if diff:
        stat = pltpu.VMEM((2, 1, tile), F32)
        return pl.pallas_call(
            functools.partial(_diff_prompt_kernel, tile=tile, lam_init=lam_init),
            grid=grid,
            in_specs=[q_tile, k_full, vt_full, q_tile,
                      pl.BlockSpec((4, DIFF_DH), lambda bi, hi, qi: (0, 0)),
                      pl.BlockSpec((1, HEAD_W), lambda bi, hi, qi: (0, 0))],
            out_specs=q_tile, out_shape=out_shape,
            scratch_shapes=[pltpu.VMEM((2, tile, tile), F32), pltpu.VMEM((2, tile, tile), BF16),
                            stat, stat, stat, stat, pltpu.VMEM((2, HEAD_W, tile), F32)],
            compiler_params=params, name="diff_prompt",
        )(q, k, vt, g, lam_vecs, subln_w)
    sub = math.gcd(SB_SUB, tile)
    return pl.pallas_call(
        functools.partial(_sb_prompt_kernel, tile=tile, sub=sub),
        grid=grid,
        in_specs=[q_tile, k_full, vt_full, q_tile],
        out_specs=q_tile, out_shape=out_shape,
        scratch_shapes=[pltpu.VMEM((tile, tile), F32), pltpu.VMEM((tile, tile), BF16),
                        pltpu.VMEM((1, tile), F32), pltpu.VMEM((HEAD_W, tile), F32)],
        compiler_params=params, name="sb_prompt",
    )(q, k, vt, g)


def _diff_sample_kernel(q_ref, kn_ref, vn_ref, kc_ref, vc_ref, g_ref, lam_ref, sw_ref, o_ref,
                        *, past, lam_init):
    q = q_ref[0]
    kn = kn_ref[0]
    vn = vn_ref[0]
    kc = kc_ref[0, 0].astype(BF16)
    vc = vc_ref[0, 0].astype(BF16)
    t = q.shape[0]
    row = lax.broadcasted_iota(jnp.int32, (t, t), 0)
    col = lax.broadcasted_iota(jnp.int32, (t, t), 1)
    visible = ((past + col) // CHUNK) <= ((past + row) // CHUNK)

    def one_map(lo):
        qm = q[:, lo:lo + DIFF_DH]
        sp = _nt_dot(qm, kc[:, lo:lo + DIFF_DH])
        sn = jnp.where(visible, _nt_dot(qm, kn[:, lo:lo + DIFF_DH]), NEG_INF)
        m = jnp.maximum(jnp.max(sp, axis=-1, keepdims=True), jnp.max(sn, axis=-1, keepdims=True))
        pp = jnp.exp(sp - m)
        pn = jnp.exp(sn - m)
        l = jnp.sum(pp, axis=-1, keepdims=True) + jnp.sum(pn, axis=-1, keepdims=True)
        return (_dot(pp.astype(BF16), vc) + _dot(pn.astype(BF16), vn)) / l

    lam = _diff_lambda(lam_ref, lam_init)
    o = one_map(0) - lam * one_map(DIFF_DH)
    o_ref[0] = _diff_finish(o, sw_ref[...], g_ref[0].astype(F32), lam_init)


def _sb_sample_kernel(q_ref, kn_ref, vn_ref, kc_ref, vc_ref, g_ref, o_ref, *, past, tk):
    q = q_ref[0]
    t = q.shape[0]
    row = lax.broadcasted_iota(jnp.int32, (t, t), 0)
    col = lax.broadcasted_iota(jnp.int32, (t, t), 1)
    earlier = col < row
    z = _nt_dot(q, kn_ref[0])
    sp, ls = _break_logs(z)
    sp = jnp.where(earlier, sp, 0.0).astype(BF16)
    w = _dot(sp, _later_ones(t))
    a = jnp.where(earlier, jnp.exp(ls - w), 0.0)
    acc = _dot(a.astype(BF16), vn_ref[0])
    c = w[:, 0:1] + sp[:, 0:1].astype(F32)
    tri = _later_ones(tk)
    for kb in reversed(range(past // tk)):
        ks = kc_ref[0, 0, kb * tk:(kb + 1) * tk, :].astype(BF16)
        vs = vc_ref[0, 0, kb * tk:(kb + 1) * tk, :].astype(BF16)
        sp, ls = _break_logs(_nt_dot(q, ks))
        sp = sp.astype(BF16)
        w = _dot(sp, tri)
        acc = acc + _dot(jnp.exp(ls - w - c).astype(BF16), vs)
        c = c + w[:, 0:1] + sp[:, 0:1].astype(F32)
    o_ref[0] = (acc * g_ref[0].astype(F32)).astype(BF16)


def _sample_attention(q, kn, vn, cache_k, cache_v, layer, g, lam_vecs, subln_w, *, diff, lam_init):
    b, t, width = q.shape
    past = cache_k.shape[2]
    heads = width // HEAD_W
    tile = pl.BlockSpec((1, t, HEAD_W), lambda bi, hi: (bi, 0, hi))
    cache = pl.BlockSpec((1, 1, past, HEAD_W), lambda bi, hi: (layer, bi, 0, hi))
    params = pltpu.CompilerParams(dimension_semantics=("arbitrary",) * 2,
                                  vmem_limit_bytes=V7X_VMEM_LIMIT)
    out_shape = jax.ShapeDtypeStruct((b, t, width), BF16)
    if diff:
        return pl.pallas_call(
            functools.partial(_diff_sample_kernel, past=past, lam_init=lam_init),
            grid=(b, heads),
            in_specs=[tile, tile, tile, cache, cache, tile,
                      pl.BlockSpec((4, DIFF_DH), lambda bi, hi: (0, 0)),
                      pl.BlockSpec((1, HEAD_W), lambda bi, hi: (0, 0))],
            out_specs=tile, out_shape=out_shape, compiler_params=params, name="diff_sample",
        )(q, kn, vn, cache_k, cache_v, g, lam_vecs, subln_w)
    tk = math.gcd(past, V7X_MXU_DEPTH)
    return pl.pallas_call(
        functools.partial(_sb_sample_kernel, past=past, tk=tk),
        grid=(b, heads),
        in_specs=[tile, tile, tile, cache, cache, tile],
        out_specs=tile, out_shape=out_shape, compiler_params=params, name="sb_sample",
    )(q, kn, vn, cache_k, cache_v, g)


def _rope_tables(positions):
    half = DIFF_DH // 2
    inv = ROPE_THETA ** (-jnp.arange(half, dtype=F32) / half)
    ang = positions.astype(F32)[:, None] * inv[None, :]
    cos = jnp.cos(ang)
    sin = jnp.sin(ang)
    return jnp.concatenate([cos, cos], axis=1), jnp.concatenate([-sin, sin], axis=1)


def kernel(x_prompt, x_sample, cache_k, cache_v, norm_w, w_in, w_out, diff_lambda, diff_subln_w,
           final_norm_w):
    depth, d_model = norm_w.shape
    b, t, _ = x_prompt.shape
    sb, st, _ = x_sample.shape
    past = cache_k.shape[2]
    width = w_out.shape[1]
    assert width == DIFF_HEADS * 2 * DIFF_DH == SB_HEADS * SB_DH
    assert past % CHUNK == 0 and st % 16 == 0

    tile = min(ATTN_TILE, t)
    tm_p = tile
    tm_s = min(ROW_TILE, sb * st)
    assert t % tile == 0 and tile % CHUNK == 0 and (sb * st) % tm_s == 0 and tm_s % st == 0

    cos_p, sin_p = _rope_tables(jnp.arange(t, dtype=jnp.int32))
    cos_s, sin_s = _rope_tables(past + jnp.arange(st, dtype=jnp.int32))
    cos_s = jnp.tile(cos_s, (tm_s // st, 1))
    sin_s = jnp.tile(sin_s, (tm_s // st, 1))

    w_in_bf = w_in.astype(BF16)
    w_out_bf = w_out.astype(BF16)
    final_w = final_norm_w.reshape(1, d_model)

    yp = x_prompt.reshape(b * t, d_model)
    ys = x_sample.reshape(sb * st, d_model)
    stacks_p = stacks_s = None
    for i in range(depth):
        diff = i % 2 == 0
        j = i // 2
        lam_init = 0.8 - 0.6 * math.exp(-0.3 * i)
        q_scale = (DIFF_DH if diff else SB_DH) ** -0.5
        nw = norm_w[i].reshape(1, d_model)
        lam_vecs = diff_lambda[j]
        subln = diff_subln_w[j].reshape(1, HEAD_W)
        final = i == depth - 1

        q, kst, vst, kb, vt, g = _project(yp, nw, w_in_bf[i], cos_p, sin_p, stacks_p, i,
                                          depth, rope=diff, q_scale=q_scale * LOG2E, tm=tm_p,
                                          v_transposed=True)
        stacks_p = (kst, vst)
        shp = (b, t, width)
        o = _prompt_attention(q.reshape(shp), kb.reshape(shp), vt, g.reshape(shp),
                              lam_vecs, subln, diff=diff, lam_init=lam_init, tile=tile)
        yp = _out_project(o.reshape(b * t, width), w_out_bf[i], yp, final_w, final=final, tm=tm_p)

        q, kst, vst, kb, vb, g = _project(ys, nw, w_in_bf[i], cos_s, sin_s, stacks_s, i,
                                          depth, rope=diff, q_scale=q_scale, tm=tm_s,
                                          v_transposed=False)
        stacks_s = (kst, vst)
        shp = (sb, st, width)
        o = _sample_attention(q.reshape(shp), kb.reshape(shp), vb.reshape(shp), cache_k, cache_v, i,
                              g.reshape(shp), lam_vecs, subln, diff=diff, lam_init=lam_init)
        ys = _out_project(o.reshape(sb * st, width), w_out_bf[i], ys, final_w, final=final, tm=tm_s)

    return (yp.reshape(b, t, d_model), ys.reshape(sb, st, d_model),
            stacks_p[0].reshape(depth, b, t, width), stacks_p[1].reshape(depth, b, t, width),
            stacks_s[0].reshape(depth, sb, st, width), stacks_s[1].reshape(depth, sb, st, width))
```

```python
import functools
import math

import jax
import jax.numpy as jnp
from jax import lax
from jax.experimental import pallas as pl
from jax.experimental.pallas import tpu as pltpu

F32 = jnp.float32
BF16 = jnp.bfloat16

CHUNK = 64
DIFF_HEADS = 4
DIFF_DH = 128
SB_HEADS = 4
SB_DH = 256
HEAD_W = 256
ROPE_THETA = 10000.0
EPS = 1e-6
NEG_INF = -1e30
LOG2E = math.log2(math.e)

V7X_VMEM_LIMIT = 56 * 1024 * 1024
V7X_MXU_DEPTH = 256
ROW_TILE = 512
ATTN_TILE = 512
SB_SUB = V7X_MXU_DEPTH
DIFF_UNROLL = 2
SB_ZERO_MASS = 160.0


def _unrolled_loop(n, body, unroll):
    main = lax.shift_right_logical(n, int(math.log2(unroll)))

    def group(j, carry):
        for u in range(unroll):
            carry = body(j * unroll + u, carry)
        return carry

    lax.fori_loop(0, main, group, 0)
    lax.fori_loop(main * unroll, n, body, 0)


def _nt_dot(a, b):
    return lax.dot_general(a, b, (((1,), (1,)), ((), ())), preferred_element_type=F32)


def _dot(a, b):
    return jnp.dot(a, b, preferred_element_type=F32)


def _rms(x, w):
    return x * lax.rsqrt(jnp.mean(x * x, axis=-1, keepdims=True) + EPS) * w


def _rope_tile(x, cos, sin_signed):
    outs = []
    for g in range(x.shape[1] // DIFF_DH):
        xg = x[:, g * DIFF_DH:(g + 1) * DIFF_DH]
        outs.append(xg * cos + pltpu.roll(xg, DIFF_DH // 2, axis=1) * sin_signed)
    return jnp.concatenate(outs, axis=1)


def _proj_kernel(y_ref, nw_ref, w_ref, cos_ref, sin_ref, *refs,
                 rope, q_scale, width, v_transposed):
    q_ref, kf_ref, vf_ref, kb_ref, vb_ref, g_ref = refs[-6:]
    h = _rms(y_ref[...], nw_ref[...]).astype(BF16)
    q = _dot(h, w_ref[:, 0 * width:1 * width])
    k = _dot(h, w_ref[:, 1 * width:2 * width])
    v = _dot(h, w_ref[:, 2 * width:3 * width])
    g = _dot(h, w_ref[:, 3 * width:4 * width])
    if rope:
        cos = cos_ref[...]
        sin = sin_ref[...]
        q = _rope_tile(q, cos, sin)
        k = _rope_tile(k, cos, sin)
    q_ref[...] = (q * q_scale).astype(BF16)
    kf_ref[...] = k
    kb_ref[...] = k.astype(BF16)
    vf_ref[...] = v
    if v_transposed:
        vb_ref[0] = v.T.astype(BF16)
    else:
        vb_ref[...] = v.astype(BF16)
    g_ref[...] = (g * jax.nn.sigmoid(g)).astype(BF16)


def _project(y, norm_w, w_in_bf, cos_t, sin_t, stacks, layer, depth, *,
             rope, q_scale, tm, v_transposed):
    rows, d = y.shape
    width = w_in_bf.shape[2] // 4
    n_pos_tiles = cos_t.shape[0] // tm
    row_spec = lambda w: pl.BlockSpec((tm, w), lambda i: (i, 0))
    tab_spec = pl.BlockSpec((tm, DIFF_DH), lambda i: (i % n_pos_tiles, 0))
    bf = jax.ShapeDtypeStruct((rows, width), BF16)
    stack_shape = jax.ShapeDtypeStruct((depth, rows, width), F32)
    stack_spec = pl.BlockSpec((None, tm, width), lambda i: (layer, i, 0))
    if v_transposed:
        vb_shape = jax.ShapeDtypeStruct((rows // tm, width, tm), BF16)
        vb_spec = pl.BlockSpec((1, width, tm), lambda i: (i, 0, 0))
    else:
        vb_shape, vb_spec = bf, row_spec(width)
    resident = functools.partial(pl.BlockSpec, pipeline_mode=pl.Buffered(1))
    in_specs = [row_spec(d),
                pl.BlockSpec((1, d), lambda i: (0, 0)),
                resident((None, d, 4 * width), lambda i: (layer, 0, 0)),
                tab_spec, tab_spec]
    args = [y, norm_w, w_in_bf, cos_t, sin_t]
    aliases = {}
    if stacks is not None:
        aliases = {len(args): 1, len(args) + 1: 2}
        in_specs += [pl.BlockSpec(memory_space=pl.ANY)] * 2
        args += list(stacks)
    return pl.pallas_call(
        functools.partial(_proj_kernel, rope=rope, q_scale=q_scale, width=width,
                          v_transposed=v_transposed),
        grid=(rows // tm,),
        in_specs=in_specs,
        out_specs=[row_spec(width), stack_spec, stack_spec, row_spec(width), vb_spec, row_spec(width)],
        out_shape=[bf, stack_shape, stack_shape, bf, vb_shape, bf],
        input_output_aliases=aliases,
        compiler_params=pltpu.CompilerParams(
            dimension_semantics=("arbitrary",), vmem_limit_bytes=V7X_VMEM_LIMIT),
        name="in_proj",
    )(*args)


def _out_kernel(o_ref, w_ref, y_ref, fw_ref, out_ref, *, final):
    y = y_ref[...] + _dot(o_ref[...], w_ref[...])
    if final:
        y = _rms(y, fw_ref[...])
    out_ref[...] = y


def _out_project(o_bf, w_out_bf, layer, y, final_w, *, final, tm):
    rows, d = y.shape
    width = o_bf.shape[1]
    return pl.pallas_call(
        functools.partial(_out_kernel, final=final),
        grid=(rows // tm,),
        in_specs=[pl.BlockSpec((tm, width), lambda i: (i, 0)),
                  pl.BlockSpec((None, width, d), lambda i: (layer, 0, 0)),
                  pl.BlockSpec((tm, d), lambda i: (i, 0)),
                  pl.BlockSpec((1, d), lambda i: (0, 0))],
        out_specs=pl.BlockSpec((tm, d), lambda i: (i, 0)),
        out_shape=jax.ShapeDtypeStruct((rows, d), F32),
        compiler_params=pltpu.CompilerParams(
            dimension_semantics=("arbitrary",), vmem_limit_bytes=V7X_VMEM_LIMIT),
        name="out_proj",
    )(o_bf, w_out_bf, y, final_w)


def _diff_lambda(lam_ref, lam_init):
    lf = lam_ref[...]
    a = jnp.sum(lf[0:1] * lf[1:2], axis=-1, keepdims=True)
    b = jnp.sum(lf[2:3] * lf[3:4], axis=-1, keepdims=True)
    return jnp.exp(a) - jnp.exp(b) + lam_init


def _diff_finish(o, sw, gate, lam_init):
    o = _rms(o, sw) * (1.0 - lam_init)
    return (o * gate).astype(BF16)


def _neg_abs(x):
    bits = lax.bitcast_convert_type(x, jnp.uint32) | jnp.uint32(0x80000000)
    return lax.bitcast_convert_type(bits, F32)


def _break_logs(z):
    lg = jnp.log(1.0 + jnp.exp(-jnp.abs(z)))
    return jnp.maximum(z, 0.0) + lg, jnp.minimum(z, 0.0) - lg


def _later_ones(n):
    j = lax.broadcasted_iota(jnp.int32, (n, n), 0)
    s = lax.broadcasted_iota(jnp.int32, (n, n), 1)
    return (j > s).astype(BF16)


def _diff_prompt_kernel(q_ref, k_ref, vt_ref, g_ref, lam_ref, sw_ref, o_ref,
                        s_ref, p_ref, mx_ref, al_ref, m_ref, l_ref, acc_ref, *, tile, lam_init):
    qi = pl.program_id(2)
    q = q_ref[0]
    qm = (q[:, :DIFF_DH], q[:, DIFF_DH:])
    m_ref[...] = jnp.full(m_ref.shape, NEG_INF, F32)
    l_ref[...] = jnp.zeros(l_ref.shape, F32)
    acc_ref[...] = jnp.zeros(acc_ref.shape, F32)

    def scores(kb):
        start = pl.multiple_of(kb * tile, tile)
        ks = k_ref[0, pl.ds(start, tile), :]
        for i in range(2):
            s = _nt_dot(ks[:, i * DIFF_DH:(i + 1) * DIFF_DH], qm[i])
            s_ref[i] = s
            mx_ref[i] = jnp.max(s, axis=0, keepdims=True)

    def softmax(visible):
        for i in range(2):
            s = s_ref[i]
            if visible is None:
                mx = mx_ref[i]
            else:
                s = jnp.where(visible, s, NEG_INF)
                mx = jnp.max(s, axis=0, keepdims=True)
            m_old = m_ref[i]
            m_new = jnp.maximum(m_old, mx)
            p = jnp.exp2(s - m_new)
            alpha = jnp.exp2(m_old - m_new)
            l_ref[i] = alpha * l_ref[i] + jnp.sum(p, axis=0, keepdims=True)
            p_ref[i] = p.astype(BF16)
            al_ref[i] = alpha
            m_ref[i] = m_new

    def values(kb):
        vt = vt_ref[kb]
        for i in range(2):
            acc_ref[i] = al_ref[i] * acc_ref[i] + _dot(vt, p_ref[i])

    key = lax.broadcasted_iota(jnp.int32, (tile, tile), 0)
    qry = lax.broadcasted_iota(jnp.int32, (tile, tile), 1)
    scores(qi)
    softmax((key // CHUNK) <= (qry // CHUNK))
    scores(0)

    def body(i, carry):
        values(jnp.where(i == 0, qi, i - 1))
        softmax(None)
        scores(jnp.minimum(i + 1, qi))
        return carry

    _unrolled_loop(qi, body, DIFF_UNROLL)
    values(jnp.maximum(qi - 1, 0))

    lam = _diff_lambda(lam_ref, lam_init)
    ot = acc_ref[0] / l_ref[0] - lam * (acc_ref[1] / l_ref[1])
    o_ref[0] = _diff_finish(ot.T, sw_ref[...], g_ref[0].astype(F32), lam_init)


def _sb_prompt_kernel(q_ref, k_ref, vt_ref, g_ref, o_ref, z_ref, a_ref, c_ref, acc_ref, *, tile, sub):
    qi = pl.program_id(2)
    q = q_ref[0]
    j = lax.broadcasted_iota(jnp.int32, (sub, sub), 0)
    s = lax.broadcasted_iota(jnp.int32, (sub, sub), 1)
    upper = (s > j).astype(BF16)
    c_ref[...] = jnp.zeros(c_ref.shape, F32)
    acc_ref[...] = jnp.zeros(acc_ref.shape, F32)

    def logits(kb):
        start = pl.multiple_of(kb * tile, tile)
        z_ref[...] = _nt_dot(k_ref[0, pl.ds(start, tile), :], q)

    def weights(diagonal):
        c = c_ref[...]
        for d in reversed(range(tile // sub)):
            z = z_ref[d * sub:(d + 1) * sub, :]
            sp = jnp.maximum(z, 0.0) + jnp.log2(1.0 + jnp.exp2(_neg_abs(z)))
            ls = z - sp
            if diagonal:
                key = lax.broadcasted_iota(jnp.int32, (sub, tile), 0) + d * sub
                qry = lax.broadcasted_iota(jnp.int32, (sub, tile), 1)
                earlier = key < qry
                sp = jnp.where(earlier, sp, 0.0)
            sp = sp.astype(BF16)
            w = _dot(upper, sp)
            a = jnp.exp2(ls - w - c)
            if diagonal:
                a = jnp.where(earlier, a, 0.0)
            a_ref[d * sub:(d + 1) * sub, :] = a.astype(BF16)
            c = c + w[0:1, :] + sp[0:1, :].astype(F32)
        c_ref[...] = c

    def values(kb):
        acc_ref[...] += _dot(vt_ref[kb], a_ref[...])

    def more_weight_possible():
        return (jnp.min(c_ref[...]) < SB_ZERO_MASS).astype(jnp.int32)

    logits(qi)
    weights(True)
    logits(jnp.maximum(qi - 1, 0))

    def cond(state):
        i, go = state
        return jnp.logical_and(i < qi, go > 0)

    def body(state):
        i, _ = state
        kb = qi - 1 - i
        values(kb + 1)
        weights(False)
        logits(jnp.maximum(kb - 1, 0))
        return i + 1, more_weight_possible()

    visited, _ = lax.while_loop(cond, body, (jnp.int32(0), more_weight_possible()))
    values(qi - visited)
    o_ref[0] = (acc_ref[...].T * g_ref[0].astype(F32)).astype(BF16)


def _prompt_attention(q, k, vt, g, lam_vecs, subln_w, *, diff, lam_init, tile):
    b, t, width = q.shape
    heads = width // HEAD_W
    nk = t // tile
    vt = vt.reshape(b * nk, width, tile)
    q_tile = pl.BlockSpec((1, tile, HEAD_W), lambda bi, hi, qi: (bi, qi, hi))
    k_full = pl.BlockSpec((1, t, HEAD_W), lambda bi, hi, qi: (bi, 0, hi))
    vt_full = pl.BlockSpec((nk, HEAD_W, tile), lambda bi, hi, qi: (bi, hi, 0))
    params = pltpu.CompilerParams(dimension_semantics=("arbitrary",) * 3,
                                  vmem_limit_bytes=V7X_VMEM_LIMIT)
    out_shape = jax.ShapeDtypeStruct((b, t, width), BF16)
    grid = (b, heads, nk)
    if diff:
        stat = pltpu.VMEM((2, 1, tile), F32)
        return pl.pallas_call(
            functools.partial(_diff_prompt_kernel, tile=tile, lam_init=lam_init),
            grid=grid,
            in_specs=[q_tile, k_full, vt_full, q_tile,
                      pl.BlockSpec((4, DIFF_DH), lambda bi, hi, qi: (0, 0)),
                      pl.BlockSpec((1, HEAD_W), lambda bi, hi, qi: (0, 0))],
            out_specs=q_tile, out_shape=out_shape,
            scratch_shapes=[pltpu.VMEM((2, tile, tile), F32), pltpu.VMEM((2, tile, tile), BF16),
                            stat, stat, stat, stat, pltpu.VMEM((2, HEAD_W, tile), F32)],
            compiler_params=params, name="diff_prompt",
        )(q, k, vt, g, lam_vecs, subln_w)
    sub = math.gcd(SB_SUB, tile)
    return pl.pallas_call(
        functools.partial(_sb_prompt_kernel, tile=tile, sub=sub),
        grid=grid,
        in_specs=[q_tile, k_full, vt_full, q_tile],
        out_specs=q_tile, out_shape=out_shape,
        scratch_shapes=[pltpu.VMEM((tile, tile), F32), pltpu.VMEM((tile, tile), BF16),
                        pltpu.VMEM((1, tile), F32), pltpu.VMEM((HEAD_W, tile), F32)],
        compiler_params=params, name="sb_prompt",
    )(q, k, vt, g)


def _diff_sample_kernel(q_ref, kn_ref, vn_ref, kc_ref, vc_ref, g_ref, lam_ref, sw_ref, o_ref,
                        *, past, lam_init):
    q = q_ref[0]
    kn = kn_ref[0]
    vn = vn_ref[0]
    kc = kc_ref[0, 0].astype(BF16)
    vc = vc_ref[0, 0].astype(BF16)
    t = q.shape[0]
    row = lax.broadcasted_iota(jnp.int32, (t, t), 0)
    col = lax.broadcasted_iota(jnp.int32, (t, t), 1)
    visible = ((past + col) // CHUNK) <= ((past + row) // CHUNK)

    def one_map(lo):
        qm = q[:, lo:lo + DIFF_DH]
        sp = _nt_dot(qm, kc[:, lo:lo + DIFF_DH])
        sn = jnp.where(visible, _nt_dot(qm, kn[:, lo:lo + DIFF_DH]), NEG_INF)
        m = jnp.maximum(jnp.max(sp, axis=-1, keepdims=True), jnp.max(sn, axis=-1, keepdims=True))
        pp = jnp.exp(sp - m)
        pn = jnp.exp(sn - m)
        l = jnp.sum(pp, axis=-1, keepdims=True) + jnp.sum(pn, axis=-1, keepdims=True)
        return (_dot(pp.astype(BF16), vc) + _dot(pn.astype(BF16), vn)) / l

    lam = _diff_lambda(lam_ref, lam_init)
    o = one_map(0) - lam * one_map(DIFF_DH)
    o_ref[0] = _diff_finish(o, sw_ref[...], g_ref[0].astype(F32), lam_init)


def _sb_sample_kernel(q_ref, kn_ref, vn_ref, kc_ref, vc_ref, g_ref, o_ref, *, past, tk):
    q = q_ref[0]
    t = q.shape[0]
    row = lax.broadcasted_iota(jnp.int32, (t, t), 0)
    col = lax.broadcasted_iota(jnp.int32, (t, t), 1)
    earlier = col < row
    z = _nt_dot(q, kn_ref[0])
    sp, ls = _break_logs(z)
    sp = jnp.where(earlier, sp, 0.0).astype(BF16)
    w = _dot(sp, _later_ones(t))
    a = jnp.where(earlier, jnp.exp(ls - w), 0.0)
    acc = _dot(a.astype(BF16), vn_ref[0])
    c = w[:, 0:1] + sp[:, 0:1].astype(F32)
    tri = _later_ones(tk)
    for kb in reversed(range(past // tk)):
        ks = kc_ref[0, 0, kb * tk:(kb + 1) * tk, :].astype(BF16)
        vs = vc_ref[0, 0, kb * tk:(kb + 1) * tk, :].astype(BF16)
        sp, ls = _break_logs(_nt_dot(q, ks))
        sp = sp.astype(BF16)
        w = _dot(sp, tri)
        acc = acc + _dot(jnp.exp(ls - w - c).astype(BF16), vs)
        c = c + w[:, 0:1] + sp[:, 0:1].astype(F32)
    o_ref[0] = (acc * g_ref[0].astype(F32)).astype(BF16)


def _sample_attention(q, kn, vn, cache_k, cache_v, layer, g, lam_vecs, subln_w, *, diff, lam_init):
    b, t, width = q.shape
    past = cache_k.shape[2]
    heads = width // HEAD_W
    tile = pl.BlockSpec((1, t, HEAD_W), lambda bi, hi: (bi, 0, hi))
    cache = pl.BlockSpec((1, 1, past, HEAD_W), lambda bi, hi: (layer, bi, 0, hi))
    params = pltpu.CompilerParams(dimension_semantics=("arbitrary",) * 2,
                                  vmem_limit_bytes=V7X_VMEM_LIMIT)
    out_shape = jax.ShapeDtypeStruct((b, t, width), BF16)
    if diff:
        return pl.pallas_call(
            functools.partial(_diff_sample_kernel, past=past, lam_init=lam_init),
            grid=(b, heads),
            in_specs=[tile, tile, tile, cache, cache, tile,
                      pl.BlockSpec((4, DIFF_DH), lambda bi, hi: (0, 0)),
                      pl.BlockSpec((1, HEAD_W), lambda bi, hi: (0, 0))],
            out_specs=tile, out_shape=out_shape, compiler_params=params, name="diff_sample",
        )(q, kn, vn, cache_k, cache_v, g, lam_vecs, subln_w)
    tk = math.gcd(past, V7X_MXU_DEPTH)
    return pl.pallas_call(
        functools.partial(_sb_sample_kernel, past=past, tk=tk),
        grid=(b, heads),
        in_specs=[tile, tile, tile, cache, cache, tile],
        out_specs=tile, out_shape=out_shape, compiler_params=params, name="sb_sample",
    )(q, kn, vn, cache_k, cache_v, g)


def _rope_tables(positions):
    half = DIFF_DH // 2
    inv = ROPE_THETA ** (-jnp.arange(half, dtype=F32) / half)
    ang = positions.astype(F32)[:, None] * inv[None, :]
    cos = jnp.cos(ang)
    sin = jnp.sin(ang)
    return jnp.concatenate([cos, cos], axis=1), jnp.concatenate([-sin, sin], axis=1)


def kernel(x_prompt, x_sample, cache_k, cache_v, norm_w, w_in, w_out, diff_lambda, diff_subln_w,
           final_norm_w):
    depth, d_model = norm_w.shape
    b, t, _ = x_prompt.shape
    sb, st, _ = x_sample.shape
    past = cache_k.shape[2]
    width = w_out.shape[1]
    assert width == DIFF_HEADS * 2 * DIFF_DH == SB_HEADS * SB_DH
    assert past % CHUNK == 0 and st % 16 == 0

    tile = min(ATTN_TILE, t)
    tm_p = tile
    tm_s = min(ROW_TILE, sb * st)
    assert t % tile == 0 and tile % CHUNK == 0 and (sb * st) % tm_s == 0 and tm_s % st == 0

    cos_p, sin_p = _rope_tables(jnp.arange(t, dtype=jnp.int32))
    cos_s, sin_s = _rope_tables(past + jnp.arange(st, dtype=jnp.int32))
    cos_s = jnp.tile(cos_s, (tm_s // st, 1))
    sin_s = jnp.tile(sin_s, (tm_s // st, 1))

    w_in_bf = w_in.astype(BF16)
    w_out_bf = w_out.astype(BF16)
    final_w = final_norm_w.reshape(1, d_model)

    yp = x_prompt.reshape(b * t, d_model)
    ys = x_sample.reshape(sb * st, d_model)
    stacks_p = stacks_s = None
    for i in range(depth):
        diff = i % 2 == 0
        j = i // 2
        lam_init = 0.8 - 0.6 * math.exp(-0.3 * i)
        q_scale = (DIFF_DH if diff else SB_DH) ** -0.5
        nw = norm_w[i].reshape(1, d_model)
        lam_vecs = diff_lambda[j]
        subln = diff_subln_w[j].reshape(1, HEAD_W)
        final = i == depth - 1

        q, kst, vst, kb, vt, g = _project(yp, nw, w_in_bf, cos_p, sin_p, stacks_p, i,
                                          depth, rope=diff, q_scale=q_scale * LOG2E, tm=tm_p,
                                          v_transposed=True)
        stacks_p = (kst, vst)
        shp = (b, t, width)
        o = _prompt_attention(q.reshape(shp), kb.reshape(shp), vt, g.reshape(shp),
                              lam_vecs, subln, diff=diff, lam_init=lam_init, tile=tile)
        yp = _out_project(o.reshape(b * t, width), w_out_bf, i, yp, final_w, final=final, tm=tm_p)

        q, kst, vst, kb, vb, g = _project(ys, nw, w_in_bf, cos_s, sin_s, stacks_s, i,
                                          depth, rope=diff, q_scale=q_scale, tm=tm_s,
                                          v_transposed=False)
        stacks_s = (kst, vst)
        shp = (sb, st, width)
        o = _sample_attention(q.reshape(shp), kb.reshape(shp), vb.reshape(shp), cache_k, cache_v, i,
                              g.reshape(shp), lam_vecs, subln, diff=diff, lam_init=lam_init)
        ys = _out_project(o.reshape(sb * st, width), w_out_bf, i, ys, final_w, final=final, tm=tm_s)

    return (yp.reshape(b, t, d_model), ys.reshape(sb, st, d_model),
            stacks_p[0].reshape(depth, b, t, width), stacks_p[1].reshape(depth, b, t, width),
            stacks_s[0].reshape(depth, sb, st, width), stacks_s[1].reshape(depth, sb, st, width))
```

```python
import functools
import math

import jax
import jax.numpy as jnp
from jax import lax
from jax.experimental import pallas as pl
from jax.experimental.pallas import tpu as pltpu

F32 = jnp.float32
BF16 = jnp.bfloat16

CHUNK = 64
DIFF_HEADS = 4
DIFF_DH = 128
SB_HEADS = 4
SB_DH = 256
HEAD_W = 256
ROPE_THETA = 10000.0
EPS = 1e-6
NEG_INF = -1e30
LOG2E = math.log2(math.e)

V7X_VMEM_LIMIT = 56 * 1024 * 1024
V7X_MXU_DEPTH = 256
ROW_TILE = 512
ATTN_TILE = 512
SB_SUB = V7X_MXU_DEPTH
SB_ZERO_MASS = 160.0
SAMPLE_HEADS = 2
PROMPT_HEADS = 2


def _nt_dot(a, b):
    return lax.dot_general(a, b, (((1,), (1,)), ((), ())), preferred_element_type=F32)


def _dot(a, b):
    return jnp.dot(a, b, preferred_element_type=F32)


def _rms(x, w):
    return x * lax.rsqrt(jnp.mean(x * x, axis=-1, keepdims=True) + EPS) * w


def _rope_tile(x, cos, sin_signed):
    outs = []
    for g in range(x.shape[1] // DIFF_DH):
        xg = x[:, g * DIFF_DH:(g + 1) * DIFF_DH]
        outs.append(xg * cos + pltpu.roll(xg, DIFF_DH // 2, axis=1) * sin_signed)
    return jnp.concatenate(outs, axis=1)


def _proj_kernel(y_ref, nw_ref, w_ref, cos_ref, sin_ref, *refs,
                 rope, q_scale, width, v_transposed):
    q_ref, kf_ref, vf_ref, kb_ref, vb_ref, g_ref = refs[-6:]
    h = _rms(y_ref[...], nw_ref[...]).astype(BF16)
    q = _dot(h, w_ref[:, 0 * width:1 * width])
    k = _dot(h, w_ref[:, 1 * width:2 * width])
    v = _dot(h, w_ref[:, 2 * width:3 * width])
    g = _dot(h, w_ref[:, 3 * width:4 * width])
    if rope:
        cos = cos_ref[...]
        sin = sin_ref[...]
        q = _rope_tile(q, cos, sin)
        k = _rope_tile(k, cos, sin)
    q_ref[...] = (q * q_scale).astype(BF16)
    kf_ref[...] = k
    kb_ref[...] = k.astype(BF16)
    vf_ref[...] = v
    if v_transposed:
        vb_ref[0] = v.T.astype(BF16)
    else:
        vb_ref[...] = v.astype(BF16)
    g_ref[...] = (g * jax.nn.sigmoid(g)).astype(BF16)


def _project(y, norm_w, w_in_bf, cos_t, sin_t, stacks, layer, depth, *,
             rope, q_scale, tm, v_transposed):
    rows, d = y.shape
    width = w_in_bf.shape[2] // 4
    n_pos_tiles = cos_t.shape[0] // tm
    row_spec = lambda w: pl.BlockSpec((tm, w), lambda i: (i, 0))
    tab_spec = pl.BlockSpec((tm, DIFF_DH), lambda i: (i % n_pos_tiles, 0))
    bf = jax.ShapeDtypeStruct((rows, width), BF16)
    stack_shape = jax.ShapeDtypeStruct((depth, rows, width), F32)
    stack_spec = pl.BlockSpec((None, tm, width), lambda i: (layer, i, 0))
    if v_transposed:
        vb_shape = jax.ShapeDtypeStruct((rows // tm, width, tm), BF16)
        vb_spec = pl.BlockSpec((1, width, tm), lambda i: (i, 0, 0))
    else:
        vb_shape, vb_spec = bf, row_spec(width)
    resident = functools.partial(pl.BlockSpec, pipeline_mode=pl.Buffered(1))
    in_specs = [row_spec(d),
                pl.BlockSpec((1, d), lambda i: (0, 0)),
                resident((None, d, 4 * width), lambda i: (layer, 0, 0)),
                tab_spec, tab_spec]
    args = [y, norm_w, w_in_bf, cos_t, sin_t]
    aliases = {}
    if stacks is not None:
        aliases = {len(args): 1, len(args) + 1: 2}
        in_specs += [pl.BlockSpec(memory_space=pl.ANY)] * 2
        args += list(stacks)
    return pl.pallas_call(
        functools.partial(_proj_kernel, rope=rope, q_scale=q_scale, width=width,
                          v_transposed=v_transposed),
        grid=(rows // tm,),
        in_specs=in_specs,
        out_specs=[row_spec(width), stack_spec, stack_spec, row_spec(width), vb_spec, row_spec(width)],
        out_shape=[bf, stack_shape, stack_shape, bf, vb_shape, bf],
        input_output_aliases=aliases,
        compiler_params=pltpu.CompilerParams(
            dimension_semantics=("arbitrary",), vmem_limit_bytes=V7X_VMEM_LIMIT),
        name="in_proj",
    )(*args)


def _out_kernel(o_ref, w_ref, y_ref, fw_ref, out_ref, *, final):
    y = y_ref[...] + _dot(o_ref[...], w_ref[...])
    if final:
        y = _rms(y, fw_ref[...])
    out_ref[...] = y


def _out_project(o_bf, w_out_bf, layer, y, final_w, *, final, tm):
    rows, d = y.shape
    width = o_bf.shape[1]
    return pl.pallas_call(
        functools.partial(_out_kernel, final=final),
        grid=(rows // tm,),
        in_specs=[pl.BlockSpec((tm, width), lambda i: (i, 0)),
                  pl.BlockSpec((None, width, d), lambda i: (layer, 0, 0)),
                  pl.BlockSpec((tm, d), lambda i: (i, 0)),
                  pl.BlockSpec((1, d), lambda i: (0, 0))],
        out_specs=pl.BlockSpec((tm, d), lambda i: (i, 0)),
        out_shape=jax.ShapeDtypeStruct((rows, d), F32),
        compiler_params=pltpu.CompilerParams(
            dimension_semantics=("arbitrary",), vmem_limit_bytes=V7X_VMEM_LIMIT),
        name="out_proj",
    )(o_bf, w_out_bf, y, final_w)


def _diff_lambda(lam_ref, lam_init):
    lf = lam_ref[...]
    a = jnp.sum(lf[0:1] * lf[1:2], axis=-1, keepdims=True)
    b = jnp.sum(lf[2:3] * lf[3:4], axis=-1, keepdims=True)
    return jnp.exp(a) - jnp.exp(b) + lam_init


def _diff_finish(o, sw, gate, lam_init):
    o = _rms(o, sw) * (1.0 - lam_init)
    return (o * gate).astype(BF16)


def _neg_abs(x):
    bits = lax.bitcast_convert_type(x, jnp.uint32) | jnp.uint32(0x80000000)
    return lax.bitcast_convert_type(bits, F32)


def _break_logs(z):
    lg = jnp.log(1.0 + jnp.exp(-jnp.abs(z)))
    return jnp.maximum(z, 0.0) + lg, jnp.minimum(z, 0.0) - lg


def _later_ones(n):
    j = lax.broadcasted_iota(jnp.int32, (n, n), 0)
    s = lax.broadcasted_iota(jnp.int32, (n, n), 1)
    return (j > s).astype(BF16)


def _diff_prompt_kernel(q_ref, k_ref, vt_ref, g_ref, lam_ref, sw_ref, o_ref,
                        s_ref, p_ref, mx_ref, al_ref, m_ref, l_ref, acc_ref, *, tile, lam_init, heads):
    qi = pl.program_id(2)
    maps = [(h, i) for h in range(heads) for i in range(2)]
    lanes = lambda h, i: slice(h * HEAD_W + i * DIFF_DH, h * HEAD_W + (i + 1) * DIFF_DH)
    qm = [q_ref[0, :, lanes(h, i)] for h, i in maps]
    m_ref[...] = jnp.full(m_ref.shape, NEG_INF, F32)
    l_ref[...] = jnp.zeros(l_ref.shape, F32)
    acc_ref[...] = jnp.zeros(acc_ref.shape, F32)

    def scores(kb):
        start = pl.multiple_of(kb * tile, tile)
        for n, (h, i) in enumerate(maps):
            s = _nt_dot(k_ref[0, pl.ds(start, tile), lanes(h, i)], qm[n])
            s_ref[n] = s
            mx_ref[n] = jnp.max(s, axis=0, keepdims=True)

    def softmax(visible):
        for n in range(len(maps)):
            s = s_ref[n]
            if visible is None:
                mx = mx_ref[n]
            else:
                s = jnp.where(visible, s, NEG_INF)
                mx = jnp.max(s, axis=0, keepdims=True)
            m_old = m_ref[n]
            m_new = jnp.maximum(m_old, mx)
            p = jnp.exp2(s - m_new)
            alpha = jnp.exp2(m_old - m_new)
            l_ref[n] = alpha * l_ref[n] + jnp.sum(p, axis=0, keepdims=True)
            p_ref[n] = p.astype(BF16)
            al_ref[n] = alpha
            m_ref[n] = m_new

    def values(kb):
        for n, (h, i) in enumerate(maps):
            vt = vt_ref[kb, h * HEAD_W:(h + 1) * HEAD_W, :]
            acc_ref[n] = al_ref[n] * acc_ref[n] + _dot(vt, p_ref[n])

    key = lax.broadcasted_iota(jnp.int32, (tile, tile), 0)
    qry = lax.broadcasted_iota(jnp.int32, (tile, tile), 1)
    scores(qi)
    softmax((key // CHUNK) <= (qry // CHUNK))
    scores(0)

    def step(i):
        values(jnp.where(i == 0, qi, i - 1))
        softmax(None)
        scores(jnp.minimum(i + 1, qi))

    def pair(j, carry):
        step(2 * j)
        step(2 * j + 1)
        return carry

    def single(i, carry):
        step(i)
        return carry

    pairs = lax.shift_right_logical(qi, 1)
    lax.fori_loop(0, pairs, pair, 0)
    lax.fori_loop(2 * pairs, qi, single, 0)
    values(jnp.maximum(qi - 1, 0))

    lam = _diff_lambda(lam_ref, lam_init)
    for h in range(heads):
        cols = slice(h * HEAD_W, (h + 1) * HEAD_W)
        ot = acc_ref[2 * h] / l_ref[2 * h] - lam * (acc_ref[2 * h + 1] / l_ref[2 * h + 1])
        o_ref[0, :, cols] = _diff_finish(ot.T, sw_ref[...], g_ref[0, :, cols].astype(F32), lam_init)


def _sb_prompt_kernel(q_ref, k_ref, vt_ref, g_ref, o_ref, z_ref, a_ref, c_ref, acc_ref,
                      *, tile, sub, heads):
    qi = pl.program_id(2)
    cols = lambda h: slice(h * HEAD_W, (h + 1) * HEAD_W)
    q = [q_ref[0, :, cols(h)] for h in range(heads)]
    j = lax.broadcasted_iota(jnp.int32, (sub, sub), 0)
    s = lax.broadcasted_iota(jnp.int32, (sub, sub), 1)
    upper = (s > j).astype(BF16)
    c_ref[...] = jnp.zeros(c_ref.shape, F32)
    acc_ref[...] = jnp.zeros(acc_ref.shape, F32)

    def logits(kb):
        start = pl.multiple_of(kb * tile, tile)
        for h in range(heads):
            z_ref[h] = _nt_dot(k_ref[0, pl.ds(start, tile), cols(h)], q[h])

    def weights(diagonal):
        for h in range(heads):
            c = c_ref[h]
            for d in reversed(range(tile // sub)):
                z = z_ref[h, d * sub:(d + 1) * sub, :]
                sp = jnp.maximum(z, 0.0) + jnp.log2(1.0 + jnp.exp2(_neg_abs(z)))
                ls = z - sp
                if diagonal:
                    key = lax.broadcasted_iota(jnp.int32, (sub, tile), 0) + d * sub
                    qry = lax.broadcasted_iota(jnp.int32, (sub, tile), 1)
                    earlier = key < qry
                    sp = jnp.where(earlier, sp, 0.0)
                sp = sp.astype(BF16)
                w = _dot(upper, sp)
                a = jnp.exp2(ls - w - c)
                if diagonal:
                    a = jnp.where(earlier, a, 0.0)
                a_ref[h, d * sub:(d + 1) * sub, :] = a.astype(BF16)
                c = c + w[0:1, :] + sp[0:1, :].astype(F32)
            c_ref[h] = c

    def values(kb):
        for h in range(heads):
            acc_ref[h] += _dot(vt_ref[kb, cols(h), :], a_ref[h])

    def more_weight_possible():
        return (jnp.min(c_ref[...]) < SB_ZERO_MASS).astype(jnp.int32)

    logits(qi)
    weights(True)
    logits(jnp.maximum(qi - 1, 0))

    def cond(state):
        i, go = state
        return jnp.logical_and(i < qi, go > 0)

    def body(state):
        i, _ = state
        kb = qi - 1 - i
        values(kb + 1)
        weights(False)
        logits(jnp.maximum(kb - 1, 0))
        return i + 1, more_weight_possible()

    visited, _ = lax.while_loop(cond, body, (jnp.int32(0), jnp.int32(1)))
    values(qi - visited)
    for h in range(heads):
        o_ref[0, :, cols(h)] = (acc_ref[h].T * g_ref[0, :, cols(h)].astype(F32)).astype(BF16)


def _prompt_attention(q, k, vt, g, lam_vecs, subln_w, *, diff, lam_init, tile):
    b, t, width = q.shape
    heads = math.gcd(PROMPT_HEADS, width // HEAD_W)
    cols = heads * HEAD_W
    nk = t // tile
    vt = vt.reshape(b * nk, width, tile)
    resident = functools.partial(pl.BlockSpec, pipeline_mode=pl.Buffered(1))
    q_tile = pl.BlockSpec((1, tile, cols), lambda bi, hi, qi: (bi, qi, hi))
    k_full = resident((1, t, cols), lambda bi, hi, qi: (bi, 0, hi))
    vt_full = resident((nk, cols, tile), lambda bi, hi, qi: (bi, hi, 0))
    params = pltpu.CompilerParams(dimension_semantics=("arbitrary",) * 3,
                                  vmem_limit_bytes=V7X_VMEM_LIMIT)
    out_shape = jax.ShapeDtypeStruct((b, t, width), BF16)
    grid = (b, width // cols, nk)
    if diff:
        stat = pltpu.VMEM((2 * heads, 1, tile), F32)
        return pl.pallas_call(
            functools.partial(_diff_prompt_kernel, tile=tile, lam_init=lam_init, heads=heads),
            grid=grid,
            in_specs=[q_tile, k_full, vt_full, q_tile,
                      pl.BlockSpec((4, DIFF_DH), lambda bi, hi, qi: (0, 0)),
                      pl.BlockSpec((1, HEAD_W), lambda bi, hi, qi: (0, 0))],
            out_specs=q_tile, out_shape=out_shape,
            scratch_shapes=[pltpu.VMEM((2 * heads, tile, tile), F32),
                            pltpu.VMEM((2 * heads, tile, tile), BF16),
                            stat, stat, stat, stat, pltpu.VMEM((2 * heads, HEAD_W, tile), F32)],
            compiler_params=params, name="diff_prompt",
        )(q, k, vt, g, lam_vecs, subln_w)
    sub = math.gcd(SB_SUB, tile)
    return pl.pallas_call(
        functools.partial(_sb_prompt_kernel, tile=tile, sub=sub, heads=heads),
        grid=grid,
        in_specs=[q_tile, k_full, vt_full, q_tile],
        out_specs=q_tile, out_shape=out_shape,
        scratch_shapes=[pltpu.VMEM((heads, tile, tile), F32), pltpu.VMEM((heads, tile, tile), BF16),
                        pltpu.VMEM((heads, 1, tile), F32), pltpu.VMEM((heads, HEAD_W, tile), F32)],
        compiler_params=params, name="sb_prompt",
    )(q, k, vt, g)


def _diff_sample_kernel(q_ref, kn_ref, vn_ref, kc_ref, vc_ref, g_ref, lam_ref, sw_ref, o_ref,
                        *, past, lam_init, heads):
    t = q_ref.shape[1]
    row = lax.broadcasted_iota(jnp.int32, (t, t), 0)
    col = lax.broadcasted_iota(jnp.int32, (t, t), 1)
    visible = ((past + col) // CHUNK) <= ((past + row) // CHUNK)
    lam = _diff_lambda(lam_ref, lam_init)

    for h in range(heads):
        cols = slice(h * HEAD_W, (h + 1) * HEAD_W)
        q = q_ref[0, :, cols]
        kn = kn_ref[0, :, cols]
        vn = vn_ref[0, :, cols]
        kc = kc_ref[0, 0, :, cols].astype(BF16)
        vc = vc_ref[0, 0, :, cols].astype(BF16)

        def one_map(lo):
            qm = q[:, lo:lo + DIFF_DH]
            sp = _nt_dot(qm, kc[:, lo:lo + DIFF_DH])
            sn = jnp.where(visible, _nt_dot(qm, kn[:, lo:lo + DIFF_DH]), NEG_INF)
            m = jnp.maximum(jnp.max(sp, axis=-1, keepdims=True), jnp.max(sn, axis=-1, keepdims=True))
            pp = jnp.exp(sp - m)
            pn = jnp.exp(sn - m)
            l = jnp.sum(pp, axis=-1, keepdims=True) + jnp.sum(pn, axis=-1, keepdims=True)
            return (_dot(pp.astype(BF16), vc) + _dot(pn.astype(BF16), vn)) / l

        o = one_map(0) - lam * one_map(DIFF_DH)
        o_ref[0, :, cols] = _diff_finish(o, sw_ref[...], g_ref[0, :, cols].astype(F32), lam_init)


def _sb_sample_kernel(q_ref, kn_ref, vn_ref, kc_ref, vc_ref, g_ref, o_ref, *, past, tk, heads):
    t = q_ref.shape[1]
    row = lax.broadcasted_iota(jnp.int32, (t, t), 0)
    col = lax.broadcasted_iota(jnp.int32, (t, t), 1)
    earlier = col < row
    tri_new = _later_ones(t)
    tri = _later_ones(tk)
    for h in range(heads):
        cols = slice(h * HEAD_W, (h + 1) * HEAD_W)
        q = q_ref[0, :, cols]
        sp, ls = _break_logs(_nt_dot(q, kn_ref[0, :, cols]))
        sp = jnp.where(earlier, sp, 0.0).astype(BF16)
        w = _dot(sp, tri_new)
        a = jnp.where(earlier, jnp.exp(ls - w), 0.0)
        acc = _dot(a.astype(BF16), vn_ref[0, :, cols])
        c = w[:, 0:1] + sp[:, 0:1].astype(F32)
        for kb in reversed(range(past // tk)):
            ks = kc_ref[0, 0, kb * tk:(kb + 1) * tk, cols].astype(BF16)
            vs = vc_ref[0, 0, kb * tk:(kb + 1) * tk, cols].astype(BF16)
            sp, ls = _break_logs(_nt_dot(q, ks))
            sp = sp.astype(BF16)
            w = _dot(sp, tri)
            acc = acc + _dot(jnp.exp(ls - w - c).astype(BF16), vs)
            c = c + w[:, 0:1] + sp[:, 0:1].astype(F32)
        o_ref[0, :, cols] = (acc * g_ref[0, :, cols].astype(F32)).astype(BF16)


def _sample_attention(q, kn, vn, cache_k, cache_v, layer, g, lam_vecs, subln_w, *, diff, lam_init):
    b, t, width = q.shape
    past = cache_k.shape[2]
    heads = math.gcd(SAMPLE_HEADS, width // HEAD_W)
    cols = heads * HEAD_W
    tile = pl.BlockSpec((1, t, cols), lambda bi, hi: (bi, 0, hi))
    cache = pl.BlockSpec((1, 1, past, cols), lambda bi, hi: (layer, bi, 0, hi))
    params = pltpu.CompilerParams(dimension_semantics=("arbitrary",) * 2,
                                  vmem_limit_bytes=V7X_VMEM_LIMIT)
    out_shape = jax.ShapeDtypeStruct((b, t, width), BF16)
    grid = (b, width // cols)
    if diff:
        return pl.pallas_call(
            functools.partial(_diff_sample_kernel, past=past, lam_init=lam_init, heads=heads),
            grid=grid,
            in_specs=[tile, tile, tile, cache, cache, tile,
                      pl.BlockSpec((4, DIFF_DH), lambda bi, hi: (0, 0)),
                      pl.BlockSpec((1, HEAD_W), lambda bi, hi: (0, 0))],
            out_specs=tile, out_shape=out_shape, compiler_params=params, name="diff_sample",
        )(q, kn, vn, cache_k, cache_v, g, lam_vecs, subln_w)
    tk = math.gcd(past, V7X_MXU_DEPTH)
    return pl.pallas_call(
        functools.partial(_sb_sample_kernel, past=past, tk=tk, heads=heads),
        grid=grid,
        in_specs=[tile, tile, tile, cache, cache, tile],
        out_specs=tile, out_shape=out_shape, compiler_params=params, name="sb_sample",
    )(q, kn, vn, cache_k, cache_v, g)


def _rope_tables(positions):
    half = DIFF_DH // 2
    inv = ROPE_THETA ** (-jnp.arange(half, dtype=F32) / half)
    ang = positions.astype(F32)[:, None] * inv[None, :]
    cos = jnp.cos(ang)
    sin = jnp.sin(ang)
    return jnp.concatenate([cos, cos], axis=1), jnp.concatenate([-sin, sin], axis=1)


def kernel(x_prompt, x_sample, cache_k, cache_v, norm_w, w_in, w_out, diff_lambda, diff_subln_w,
           final_norm_w):
    depth, d_model = norm_w.shape
    b, t, _ = x_prompt.shape
    sb, st, _ = x_sample.shape
    past = cache_k.shape[2]
    width = w_out.shape[1]
    assert width == DIFF_HEADS * 2 * DIFF_DH == SB_HEADS * SB_DH
    assert past % CHUNK == 0 and st % 16 == 0

    tile = min(ATTN_TILE, t)
    tm_p = tile
    tm_s = min(ROW_TILE, sb * st)
    assert t % tile == 0 and tile % CHUNK == 0 and (sb * st) % tm_s == 0 and tm_s % st == 0

    cos_p, sin_p = _rope_tables(jnp.arange(t, dtype=jnp.int32))
    cos_s, sin_s = _rope_tables(past + jnp.arange(st, dtype=jnp.int32))
    cos_s = jnp.tile(cos_s, (tm_s // st, 1))
    sin_s = jnp.tile(sin_s, (tm_s // st, 1))

    w_in_bf = w_in.astype(BF16)
    w_out_bf = w_out.astype(BF16)
    final_w = final_norm_w.reshape(1, d_model)

    yp = x_prompt.reshape(b * t, d_model)
    ys = x_sample.reshape(sb * st, d_model)
    stacks_p = stacks_s = None
    for i in range(depth):
        diff = i % 2 == 0
        j = i // 2
        lam_init = 0.8 - 0.6 * math.exp(-0.3 * i)
        q_scale = (DIFF_DH if diff else SB_DH) ** -0.5
        nw = norm_w[i].reshape(1, d_model)
        lam_vecs = diff_lambda[j]
        subln = diff_subln_w[j].reshape(1, HEAD_W)
        final = i == depth - 1

        q, kst, vst, kb, vt, g = _project(yp, nw, w_in_bf, cos_p, sin_p, stacks_p, i,
                                          depth, rope=diff, q_scale=q_scale * LOG2E, tm=tm_p,
                                          v_transposed=True)
        stacks_p = (kst, vst)
        shp = (b, t, width)
        o = _prompt_attention(q.reshape(shp), kb.reshape(shp), vt, g.reshape(shp),
                              lam_vecs, subln, diff=diff, lam_init=lam_init, tile=tile)
        yp = _out_project(o.reshape(b * t, width), w_out_bf, i, yp, final_w, final=final, tm=tm_p)

        q, kst, vst, kb, vb, g = _project(ys, nw, w_in_bf, cos_s, sin_s, stacks_s, i,
                                          depth, rope=diff, q_scale=q_scale, tm=tm_s,
                                          v_transposed=False)
        stacks_s = (kst, vst)
        shp = (sb, st, width)
        o = _sample_attention(q.reshape(shp), kb.reshape(shp), vb.reshape(shp), cache_k, cache_v, i,
                              g.reshape(shp), lam_vecs, subln, diff=diff, lam_init=lam_init)
        ys = _out_project(o.reshape(sb * st, width), w_out_bf, i, ys, final_w, final=final, tm=tm_s)

    return (yp.reshape(b, t, d_model), ys.reshape(sb, st, d_model),
            stacks_p[0].reshape(depth, b, t, width), stacks_p[1].reshape(depth, b, t, width),
            stacks_s[0].reshape(depth, sb, st, width), stacks_s[1].reshape(depth, sb, st, width))
```

```python
import functools
import math

import jax
import jax.numpy as jnp
from jax import lax
from jax.experimental import pallas as pl
from jax.experimental.pallas import tpu as pltpu

F32 = jnp.float32
BF16 = jnp.bfloat16

CHUNK = 64
DIFF_HEADS = 4
DIFF_DH = 128
SB_HEADS = 4
SB_DH = 256
HEAD_W = 256
ROPE_THETA = 10000.0
EPS = 1e-6
NEG_INF = -1e30
LOG2E = math.log2(math.e)

V7X_VMEM_LIMIT = 56 * 1024 * 1024
V7X_MXU_DEPTH = 256
ROW_TILE = 512
ATTN_TILE = 512
SB_SUB = V7X_MXU_DEPTH
SB_ZERO_MASS = 160.0
SAMPLE_HEADS = 2
PROMPT_HEADS = 2


def _nt_dot(a, b):
    return lax.dot_general(a, b, (((1,), (1,)), ((), ())), preferred_element_type=F32)


def _dot(a, b):
    return jnp.dot(a, b, preferred_element_type=F32)


def _rms(x, w):
    return x * lax.rsqrt(jnp.mean(x * x, axis=-1, keepdims=True) + EPS) * w


def _rope_tile(x, cos, sin_signed):
    outs = []
    for g in range(x.shape[1] // DIFF_DH):
        xg = x[:, g * DIFF_DH:(g + 1) * DIFF_DH]
        outs.append(xg * cos + pltpu.roll(xg, DIFF_DH // 2, axis=1) * sin_signed)
    return jnp.concatenate(outs, axis=1)


def _proj_kernel(y_ref, nw_ref, w_ref, cos_ref, sin_ref, kst_ref, vst_ref, *refs,
                 rope, q_scale, width, v_transposed, fused):
    if fused:
        o_ref, wout_ref, *refs = refs
    q_ref, kf_ref, vf_ref, kb_ref, vb_ref, g_ref = refs[:6]
    y = y_ref[...]
    if fused:
        y = y + _dot(o_ref[...], wout_ref[...])
        refs[6][...] = y
    h = _rms(y, nw_ref[...]).astype(BF16)
    q = _dot(h, w_ref[:, 0 * width:1 * width])
    k = _dot(h, w_ref[:, 1 * width:2 * width])
    v = _dot(h, w_ref[:, 2 * width:3 * width])
    g = _dot(h, w_ref[:, 3 * width:4 * width])
    if rope:
        cos = cos_ref[...]
        sin = sin_ref[...]
        q = _rope_tile(q, cos, sin)
        k = _rope_tile(k, cos, sin)
    q_ref[...] = (q * q_scale).astype(BF16)
    kf_ref[...] = k
    kb_ref[...] = k.astype(BF16)
    vf_ref[...] = v
    if v_transposed:
        vb_ref[0] = v.T.astype(BF16)
    else:
        vb_ref[...] = v.astype(BF16)
    g_ref[...] = (g * jax.nn.sigmoid(g)).astype(BF16)


def _project(y, norm_w, w_in_bf, cos_t, sin_t, stacks, layer, depth, prev, *,
             rope, q_scale, tm, v_transposed):
    rows, d = y.shape
    width = w_in_bf.shape[2] // 4
    n_pos_tiles = cos_t.shape[0] // tm
    row_spec = lambda w: pl.BlockSpec((tm, w), lambda i: (i, 0))
    tab_spec = pl.BlockSpec((tm, DIFF_DH), lambda i: (i % n_pos_tiles, 0))
    bf = jax.ShapeDtypeStruct((rows, width), BF16)
    stack_shape = jax.ShapeDtypeStruct((depth, rows, width), F32)
    stack_spec = pl.BlockSpec((None, tm, width), lambda i: (layer, i, 0))
    if v_transposed:
        vb_shape = jax.ShapeDtypeStruct((rows // tm, width, tm), BF16)
        vb_spec = pl.BlockSpec((1, width, tm), lambda i: (i, 0, 0))
    else:
        vb_shape, vb_spec = bf, row_spec(width)
    resident = functools.partial(pl.BlockSpec, pipeline_mode=pl.Buffered(1))
    in_specs = [row_spec(d),
                pl.BlockSpec((1, d), lambda i: (0, 0)),
                resident((None, d, 4 * width), lambda i: (layer, 0, 0)),
                tab_spec, tab_spec]
    args = [y, norm_w, w_in_bf, cos_t, sin_t, *stacks]
    aliases = {len(args) - 2: 1, len(args) - 1: 2}
    in_specs += [pl.BlockSpec(memory_space=pl.ANY)] * 2
    out_specs = [row_spec(width), stack_spec, stack_spec, row_spec(width), vb_spec, row_spec(width)]
    out_shape = [bf, stack_shape, stack_shape, bf, vb_shape, bf]
    if prev is not None:
        o_bf, w_out_bf = prev
        args += [o_bf, w_out_bf]
        in_specs += [row_spec(width), resident((None, width, d), lambda i: (layer - 1, 0, 0))]
        out_specs.append(row_spec(d))
        out_shape.append(jax.ShapeDtypeStruct((rows, d), F32))
    return pl.pallas_call(
        functools.partial(_proj_kernel, rope=rope, q_scale=q_scale, width=width,
                          v_transposed=v_transposed, fused=prev is not None),
        grid=(rows // tm,),
        in_specs=in_specs,
        out_specs=out_specs,
        out_shape=out_shape,
        input_output_aliases=aliases,
        compiler_params=pltpu.CompilerParams(
            dimension_semantics=("arbitrary",), vmem_limit_bytes=V7X_VMEM_LIMIT),
        name="in_proj",
    )(*args)


def _out_kernel(o_ref, w_ref, y_ref, fw_ref, out_ref):
    out_ref[...] = _rms(y_ref[...] + _dot(o_ref[...], w_ref[...]), fw_ref[...])


def _out_project(o_bf, w_out_bf, layer, y, final_w, *, tm):
    rows, d = y.shape
    width = o_bf.shape[1]
    return pl.pallas_call(
        _out_kernel,
        grid=(rows // tm,),
        in_specs=[pl.BlockSpec((tm, width), lambda i: (i, 0)),
                  pl.BlockSpec((None, width, d), lambda i: (layer, 0, 0)),
                  pl.BlockSpec((tm, d), lambda i: (i, 0)),
                  pl.BlockSpec((1, d), lambda i: (0, 0))],
        out_specs=pl.BlockSpec((tm, d), lambda i: (i, 0)),
        out_shape=jax.ShapeDtypeStruct((rows, d), F32),
        compiler_params=pltpu.CompilerParams(
            dimension_semantics=("arbitrary",), vmem_limit_bytes=V7X_VMEM_LIMIT),
        name="out_proj",
    )(o_bf, w_out_bf, y, final_w)


def _diff_lambda(lam_ref, lam_init):
    lf = lam_ref[...]
    a = jnp.sum(lf[0:1] * lf[1:2], axis=-1, keepdims=True)
    b = jnp.sum(lf[2:3] * lf[3:4], axis=-1, keepdims=True)
    return jnp.exp(a) - jnp.exp(b) + lam_init


def _diff_finish(o, sw, gate, lam_init):
    o = _rms(o, sw) * (1.0 - lam_init)
    return (o * gate).astype(BF16)


def _neg_abs(x):
    bits = lax.bitcast_convert_type(x, jnp.uint32) | jnp.uint32(0x80000000)
    return lax.bitcast_convert_type(bits, F32)


def _break_logs(z):
    lg = jnp.log(1.0 + jnp.exp(-jnp.abs(z)))
    return jnp.maximum(z, 0.0) + lg, jnp.minimum(z, 0.0) - lg


def _later_ones(n):
    j = lax.broadcasted_iota(jnp.int32, (n, n), 0)
    s = lax.broadcasted_iota(jnp.int32, (n, n), 1)
    return (j > s).astype(BF16)


def _diff_prompt_kernel(q_ref, k_ref, vt_ref, g_ref, lam_ref, sw_ref, o_ref,
                        s_ref, p_ref, mx_ref, al_ref, m_ref, l_ref, acc_ref, *, tile, lam_init, heads):
    qi = pl.program_id(2)
    maps = [(h, i) for h in range(heads) for i in range(2)]
    lanes = lambda h, i: slice(h * HEAD_W + i * DIFF_DH, h * HEAD_W + (i + 1) * DIFF_DH)
    qm = [q_ref[0, :, lanes(h, i)] for h, i in maps]
    m_ref[...] = jnp.full(m_ref.shape, NEG_INF, F32)
    l_ref[...] = jnp.zeros(l_ref.shape, F32)
    acc_ref[...] = jnp.zeros(acc_ref.shape, F32)

    def scores(kb):
        start = pl.multiple_of(kb * tile, tile)
        for n, (h, i) in enumerate(maps):
            s = _nt_dot(k_ref[0, pl.ds(start, tile), lanes(h, i)], qm[n])
            s_ref[n] = s
            mx_ref[n] = jnp.max(s, axis=0, keepdims=True)

    def softmax(visible):
        for n in range(len(maps)):
            s = s_ref[n]
            if visible is None:
                mx = mx_ref[n]
            else:
                s = jnp.where(visible, s, NEG_INF)
                mx = jnp.max(s, axis=0, keepdims=True)
            m_old = m_ref[n]
            m_new = jnp.maximum(m_old, mx)
            p = jnp.exp2(s - m_new)
            alpha = jnp.exp2(m_old - m_new)
            l_ref[n] = alpha * l_ref[n] + jnp.sum(p, axis=0, keepdims=True)
            p_ref[n] = p.astype(BF16)
            al_ref[n] = alpha
            m_ref[n] = m_new

    def values(kb):
        for n, (h, i) in enumerate(maps):
            vt = vt_ref[kb, h * HEAD_W:(h + 1) * HEAD_W, :]
            acc_ref[n] = al_ref[n] * acc_ref[n] + _dot(vt, p_ref[n])

    key = lax.broadcasted_iota(jnp.int32, (tile, tile), 0)
    qry = lax.broadcasted_iota(jnp.int32, (tile, tile), 1)
    scores(qi)
    softmax((key // CHUNK) <= (qry // CHUNK))
    scores(0)

    def step(i):
        values(jnp.where(i == 0, qi, i - 1))
        softmax(None)
        scores(jnp.minimum(i + 1, qi))

    def pair(j, carry):
        step(2 * j)
        step(2 * j + 1)
        return carry

    def single(i, carry):
        step(i)
        return carry

    pairs = lax.shift_right_logical(qi, 1)
    lax.fori_loop(0, pairs, pair, 0)
    lax.fori_loop(2 * pairs, qi, single, 0)
    values(jnp.maximum(qi - 1, 0))

    lam = _diff_lambda(lam_ref, lam_init)
    for h in range(heads):
        cols = slice(h * HEAD_W, (h + 1) * HEAD_W)
        ot = acc_ref[2 * h] / l_ref[2 * h] - lam * (acc_ref[2 * h + 1] / l_ref[2 * h + 1])
        o_ref[0, :, cols] = _diff_finish(ot.T, sw_ref[...], g_ref[0, :, cols].astype(F32), lam_init)


def _sb_prompt_kernel(q_ref, k_ref, vt_ref, g_ref, o_ref, z_ref, a_ref, c_ref, acc_ref,
                      *, tile, sub, heads):
    qi = pl.program_id(2)
    cols = lambda h: slice(h * HEAD_W, (h + 1) * HEAD_W)
    q = [q_ref[0, :, cols(h)] for h in range(heads)]
    j = lax.broadcasted_iota(jnp.int32, (sub, sub), 0)
    s = lax.broadcasted_iota(jnp.int32, (sub, sub), 1)
    upper = (s > j).astype(BF16)
    c_ref[...] = jnp.zeros(c_ref.shape, F32)
    acc_ref[...] = jnp.zeros(acc_ref.shape, F32)

    def logits(kb):
        start = pl.multiple_of(kb * tile, tile)
        for h in range(heads):
            z_ref[h] = _nt_dot(k_ref[0, pl.ds(start, tile), cols(h)], q[h])

    def weights(diagonal):
        for h in range(heads):
            c = c_ref[h]
            for d in reversed(range(tile // sub)):
                z = z_ref[h, d * sub:(d + 1) * sub, :]
                sp = jnp.maximum(z, 0.0) + jnp.log2(1.0 + jnp.exp2(_neg_abs(z)))
                ls = z - sp
                if diagonal:
                    key = lax.broadcasted_iota(jnp.int32, (sub, tile), 0) + d * sub
                    qry = lax.broadcasted_iota(jnp.int32, (sub, tile), 1)
                    earlier = key < qry
                    sp = jnp.where(earlier, sp, 0.0)
                sp = sp.astype(BF16)
                w = _dot(upper, sp)
                a = jnp.exp2(ls - w - c)
                if diagonal:
                    a = jnp.where(earlier, a, 0.0)
                a_ref[h, d * sub:(d + 1) * sub, :] = a.astype(BF16)
                c = c + w[0:1, :] + sp[0:1, :].astype(F32)
            c_ref[h] = c

    def values(kb):
        for h in range(heads):
            acc_ref[h] += _dot(vt_ref[kb, cols(h), :], a_ref[h])

    def more_weight_possible():
        return (jnp.min(c_ref[...]) < SB_ZERO_MASS).astype(jnp.int32)

    logits(qi)
    weights(True)
    logits(jnp.maximum(qi - 1, 0))

    def cond(state):
        i, go = state
        return jnp.logical_and(i < qi, go > 0)

    def body(state):
        i, _ = state
        kb = qi - 1 - i
        values(kb + 1)
        weights(False)
        logits(jnp.maximum(kb - 1, 0))
        return i + 1, more_weight_possible()

    visited, _ = lax.while_loop(cond, body, (jnp.int32(0), jnp.int32(1)))
    values(qi - visited)
    for h in range(heads):
        o_ref[0, :, cols(h)] = (acc_ref[h].T * g_ref[0, :, cols(h)].astype(F32)).astype(BF16)


def _prompt_attention(q, k, vt, g, lam_vecs, subln_w, *, diff, lam_init, tile):
    b, t, width = q.shape
    heads = math.gcd(PROMPT_HEADS, width // HEAD_W)
    cols = heads * HEAD_W
    nk = t // tile
    vt = vt.reshape(b * nk, width, tile)
    resident = functools.partial(pl.BlockSpec, pipeline_mode=pl.Buffered(1))
    q_tile = pl.BlockSpec((1, tile, cols), lambda bi, hi, qi: (bi, qi, hi))
    k_full = resident((1, t, cols), lambda bi, hi, qi: (bi, 0, hi))
    vt_full = resident((nk, cols, tile), lambda bi, hi, qi: (bi, hi, 0))
    params = pltpu.CompilerParams(dimension_semantics=("arbitrary",) * 3,
                                  vmem_limit_bytes=V7X_VMEM_LIMIT)
    out_shape = jax.ShapeDtypeStruct((b, t, width), BF16)
    grid = (b, width // cols, nk)
    if diff:
        stat = pltpu.VMEM((2 * heads, 1, tile), F32)
        return pl.pallas_call(
            functools.partial(_diff_prompt_kernel, tile=tile, lam_init=lam_init, heads=heads),
            grid=grid,
            in_specs=[q_tile, k_full, vt_full, q_tile,
                      pl.BlockSpec((4, DIFF_DH), lambda bi, hi, qi: (0, 0)),
                      pl.BlockSpec((1, HEAD_W), lambda bi, hi, qi: (0, 0))],
            out_specs=q_tile, out_shape=out_shape,
            scratch_shapes=[pltpu.VMEM((2 * heads, tile, tile), F32),
                            pltpu.VMEM((2 * heads, tile, tile), BF16),
                            stat, stat, stat, stat, pltpu.VMEM((2 * heads, HEAD_W, tile), F32)],
            compiler_params=params, name="diff_prompt",
        )(q, k, vt, g, lam_vecs, subln_w)
    sub = math.gcd(SB_SUB, tile)
    return pl.pallas_call(
        functools.partial(_sb_prompt_kernel, tile=tile, sub=sub, heads=heads),
        grid=grid,
        in_specs=[q_tile, k_full, vt_full, q_tile],
        out_specs=q_tile, out_shape=out_shape,
        scratch_shapes=[pltpu.VMEM((heads, tile, tile), F32), pltpu.VMEM((heads, tile, tile), BF16),
                        pltpu.VMEM((heads, 1, tile), F32), pltpu.VMEM((heads, HEAD_W, tile), F32)],
        compiler_params=params, name="sb_prompt",
    )(q, k, vt, g)


def _diff_sample_kernel(q_ref, kn_ref, vn_ref, kc_ref, vc_ref, g_ref, lam_ref, sw_ref, o_ref,
                        *, past, lam_init, heads):
    t = q_ref.shape[1]
    row = lax.broadcasted_iota(jnp.int32, (t, t), 0)
    col = lax.broadcasted_iota(jnp.int32, (t, t), 1)
    visible = ((past + col) // CHUNK) <= ((past + row) // CHUNK)
    lam = _diff_lambda(lam_ref, lam_init)

    for h in range(heads):
        cols = slice(h * HEAD_W, (h + 1) * HEAD_W)
        q = q_ref[0, :, cols]
        kn = kn_ref[0, :, cols]
        vn = vn_ref[0, :, cols]
        kc = kc_ref[0, 0, :, cols].astype(BF16)
        vc = vc_ref[0, 0, :, cols].astype(BF16)

        def one_map(lo):
            qm = q[:, lo:lo + DIFF_DH]
            sp = _nt_dot(qm, kc[:, lo:lo + DIFF_DH])
            sn = jnp.where(visible, _nt_dot(qm, kn[:, lo:lo + DIFF_DH]), NEG_INF)
            m = jnp.maximum(jnp.max(sp, axis=-1, keepdims=True), jnp.max(sn, axis=-1, keepdims=True))
            pp = jnp.exp(sp - m)
            pn = jnp.exp(sn - m)
            l = jnp.sum(pp, axis=-1, keepdims=True) + jnp.sum(pn, axis=-1, keepdims=True)
            return (_dot(pp.astype(BF16), vc) + _dot(pn.astype(BF16), vn)) / l

        o = one_map(0) - lam * one_map(DIFF_DH)
        o_ref[0, :, cols] = _diff_finish(o, sw_ref[...], g_ref[0, :, cols].astype(F32), lam_init)


def _sb_sample_kernel(q_ref, kn_ref, vn_ref, kc_ref, vc_ref, g_ref, o_ref, *, past, tk, heads):
    t = q_ref.shape[1]
    row = lax.broadcasted_iota(jnp.int32, (t, t), 0)
    col = lax.broadcasted_iota(jnp.int32, (t, t), 1)
    earlier = col < row
    tri_new = _later_ones(t)
    tri = _later_ones(tk)
    for h in range(heads):
        cols = slice(h * HEAD_W, (h + 1) * HEAD_W)
        q = q_ref[0, :, cols]
        sp, ls = _break_logs(_nt_dot(q, kn_ref[0, :, cols]))
        sp = jnp.where(earlier, sp, 0.0).astype(BF16)
        w = _dot(sp, tri_new)
        a = jnp.where(earlier, jnp.exp(ls - w), 0.0)
        acc = _dot(a.astype(BF16), vn_ref[0, :, cols])
        c = w[:, 0:1] + sp[:, 0:1].astype(F32)
        for kb in reversed(range(past // tk)):
            ks = kc_ref[0, 0, kb * tk:(kb + 1) * tk, cols].astype(BF16)
            vs = vc_ref[0, 0, kb * tk:(kb + 1) * tk, cols].astype(BF16)
            sp, ls = _break_logs(_nt_dot(q, ks))
            sp = sp.astype(BF16)
            w = _dot(sp, tri)
            acc = acc + _dot(jnp.exp(ls - w - c).astype(BF16), vs)
            c = c + w[:, 0:1] + sp[:, 0:1].astype(F32)
        o_ref[0, :, cols] = (acc * g_ref[0, :, cols].astype(F32)).astype(BF16)


def _sample_attention(q, kn, vn, cache_k, cache_v, layer, g, lam_vecs, subln_w, *, diff, lam_init):
    b, t, width = q.shape
    past = cache_k.shape[2]
    heads = math.gcd(SAMPLE_HEADS, width // HEAD_W)
    cols = heads * HEAD_W
    tile = pl.BlockSpec((1, t, cols), lambda bi, hi: (bi, 0, hi))
    cache = pl.BlockSpec((1, 1, past, cols), lambda bi, hi: (layer, bi, 0, hi))
    params = pltpu.CompilerParams(dimension_semantics=("arbitrary",) * 2,
                                  vmem_limit_bytes=V7X_VMEM_LIMIT)
    out_shape = jax.ShapeDtypeStruct((b, t, width), BF16)
    grid = (b, width // cols)
    if diff:
        return pl.pallas_call(
            functools.partial(_diff_sample_kernel, past=past, lam_init=lam_init, heads=heads),
            grid=grid,
            in_specs=[tile, tile, tile, cache, cache, tile,
                      pl.BlockSpec((4, DIFF_DH), lambda bi, hi: (0, 0)),
                      pl.BlockSpec((1, HEAD_W), lambda bi, hi: (0, 0))],
            out_specs=tile, out_shape=out_shape, compiler_params=params, name="diff_sample",
        )(q, kn, vn, cache_k, cache_v, g, lam_vecs, subln_w)
    tk = math.gcd(past, V7X_MXU_DEPTH)
    return pl.pallas_call(
        functools.partial(_sb_sample_kernel, past=past, tk=tk, heads=heads),
        grid=grid,
        in_specs=[tile, tile, tile, cache, cache, tile],
        out_specs=tile, out_shape=out_shape, compiler_params=params, name="sb_sample",
    )(q, kn, vn, cache_k, cache_v, g)


def _rope_tables(positions):
    half = DIFF_DH // 2
    inv = ROPE_THETA ** (-jnp.arange(half, dtype=F32) / half)
    ang = positions.astype(F32)[:, None] * inv[None, :]
    cos = jnp.cos(ang)
    sin = jnp.sin(ang)
    return jnp.concatenate([cos, cos], axis=1), jnp.concatenate([-sin, sin], axis=1)


def kernel(x_prompt, x_sample, cache_k, cache_v, norm_w, w_in, w_out, diff_lambda, diff_subln_w,
           final_norm_w):
    depth, d_model = norm_w.shape
    b, t, _ = x_prompt.shape
    sb, st, _ = x_sample.shape
    past = cache_k.shape[2]
    width = w_out.shape[1]
    assert width == DIFF_HEADS * 2 * DIFF_DH == SB_HEADS * SB_DH
    assert past % CHUNK == 0 and st % 16 == 0

    tile = min(ATTN_TILE, t)
    tm_p = tile
    tm_s = min(ROW_TILE, sb * st)
    assert t % tile == 0 and tile % CHUNK == 0 and (sb * st) % tm_s == 0 and tm_s % st == 0

    cos_p, sin_p = _rope_tables(jnp.arange(t, dtype=jnp.int32))
    cos_s, sin_s = _rope_tables(past + jnp.arange(st, dtype=jnp.int32))
    cos_s = jnp.tile(cos_s, (tm_s // st, 1))
    sin_s = jnp.tile(sin_s, (tm_s // st, 1))

    w_in_bf = w_in.astype(BF16)
    w_out_bf = w_out.astype(BF16)
    final_w = final_norm_w.reshape(1, d_model)

    yp = x_prompt.reshape(b * t, d_model)
    ys = x_sample.reshape(sb * st, d_model)
    stacks_p = tuple(jnp.zeros((depth, b * t, width), F32) for _ in range(2))
    stacks_s = tuple(jnp.zeros((depth, sb * st, width), F32) for _ in range(2))
    op = os_ = None
    for i in range(depth):
        diff = i % 2 == 0
        j = i // 2
        lam_init = 0.8 - 0.6 * math.exp(-0.3 * i)
        q_scale = (DIFF_DH if diff else SB_DH) ** -0.5
        nw = norm_w[i].reshape(1, d_model)
        lam_vecs = diff_lambda[j]
        subln = diff_subln_w[j].reshape(1, HEAD_W)

        res = _project(yp, nw, w_in_bf, cos_p, sin_p, stacks_p, i, depth,
                       None if i == 0 else (op, w_out_bf),
                       rope=diff, q_scale=q_scale * LOG2E, tm=tm_p, v_transposed=True)
        q, kst, vst, kb, vt, g = res[:6]
        yp = yp if i == 0 else res[6]
        stacks_p = (kst, vst)
        shp = (b, t, width)
        op = _prompt_attention(q.reshape(shp), kb.reshape(shp), vt, g.reshape(shp), lam_vecs, subln,
                               diff=diff, lam_init=lam_init, tile=tile).reshape(b * t, width)

        res = _project(ys, nw, w_in_bf, cos_s, sin_s, stacks_s, i, depth,
                       None if i == 0 else (os_, w_out_bf),
                       rope=diff, q_scale=q_scale, tm=tm_s, v_transposed=False)
        q, kst, vst, kb, vb, g = res[:6]
        ys = ys if i == 0 else res[6]
        stacks_s = (kst, vst)
        shp = (sb, st, width)
        os_ = _sample_attention(q.reshape(shp), kb.reshape(shp), vb.reshape(shp), cache_k, cache_v, i,
                                g.reshape(shp), lam_vecs, subln, diff=diff,
                                lam_init=lam_init).reshape(sb * st, width)

    yp = _out_project(op, w_out_bf, depth - 1, yp, final_w, tm=tm_p)
    ys = _out_project(os_, w_out_bf, depth - 1, ys, final_w, tm=tm_s)
    return (yp.reshape(b, t, d_model), ys.reshape(sb, st, d_model),
            stacks_p[0].reshape(depth, b, t, width), stacks_p[1].reshape(depth, b, t, width),
            stacks_s[0].reshape(depth, sb, st, width), stacks_s[1].reshape(depth, sb, st, width))
```

```python
import functools
import math

import jax
import jax.numpy as jnp
from jax import lax
from jax.experimental import pallas as pl
from jax.experimental.pallas import tpu as pltpu

F32 = jnp.float32
BF16 = jnp.bfloat16

CHUNK = 64
DIFF_HEADS = 4
DIFF_DH = 128
SB_HEADS = 4
SB_DH = 256
HEAD_W = 256
ROPE_THETA = 10000.0
EPS = 1e-6
NEG_INF = -1e30
LOG2E = math.log2(math.e)

V7X_VMEM_LIMIT = 56 * 1024 * 1024
V7X_MXU_DEPTH = 256
ROW_TILE = 512
ATTN_TILE = 512
SB_SUB = V7X_MXU_DEPTH
SB_ZERO_MASS = 160.0
SAMPLE_HEADS = 4
PROMPT_HEADS = 2


def _nt_dot(a, b):
    return lax.dot_general(a, b, (((1,), (1,)), ((), ())), preferred_element_type=F32)


def _dot(a, b):
    return jnp.dot(a, b, preferred_element_type=F32)


def _rms(x, w):
    return x * lax.rsqrt(jnp.mean(x * x, axis=-1, keepdims=True) + EPS) * w


def _rope_tile(x, cos, sin_signed):
    outs = []
    for g in range(x.shape[1] // DIFF_DH):
        xg = x[:, g * DIFF_DH:(g + 1) * DIFF_DH]
        outs.append(xg * cos + pltpu.roll(xg, DIFF_DH // 2, axis=1) * sin_signed)
    return jnp.concatenate(outs, axis=1)


def _proj_kernel(y_ref, nw_ref, w_ref, cos_ref, sin_ref, *refs,
                 rope, q_scale, width, v_transposed, fused):
    if fused:
        _, _, o_ref, wout_ref, *refs = refs
    q_ref, kf_ref, vf_ref, kb_ref, vb_ref, g_ref = refs[:6]
    y = y_ref[...]
    if fused:
        y = y + _dot(o_ref[...], wout_ref[...])
        refs[6][...] = y
    h = _rms(y, nw_ref[...]).astype(BF16)
    q = _dot(h, w_ref[:, 0 * width:1 * width])
    k = _dot(h, w_ref[:, 1 * width:2 * width])
    v = _dot(h, w_ref[:, 2 * width:3 * width])
    g = _dot(h, w_ref[:, 3 * width:4 * width])
    if rope:
        cos = cos_ref[...]
        sin = sin_ref[...]
        q = _rope_tile(q, cos, sin)
        k = _rope_tile(k, cos, sin)
    q_ref[...] = (q * q_scale).astype(BF16)
    kb_ref[...] = k.astype(BF16)
    if fused:
        kf_ref[...] = k
        vf_ref[...] = v
    else:
        for st_ref, rows in ((kf_ref, k), (vf_ref, v)):
            st_ref[0] = rows
            st_ref[1:] = jnp.zeros((st_ref.shape[0] - 1,) + rows.shape, F32)
    if v_transposed:
        vb_ref[0] = v.T.astype(BF16)
    else:
        vb_ref[...] = v.astype(BF16)
    g_ref[...] = (g * jax.nn.sigmoid(g)).astype(BF16)


def _project(y, norm_w, w_in_bf, cos_t, sin_t, layer, depth, prev, *,
             rope, q_scale, tm, v_transposed):
    rows, d = y.shape
    width = w_in_bf.shape[2] // 4
    n_pos_tiles = cos_t.shape[0] // tm
    row_spec = lambda w: pl.BlockSpec((tm, w), lambda i: (i, 0))
    tab_spec = pl.BlockSpec((tm, DIFF_DH), lambda i: (i % n_pos_tiles, 0))
    bf = jax.ShapeDtypeStruct((rows, width), BF16)
    stack_shape = jax.ShapeDtypeStruct((depth, rows, width), F32)
    if prev is None:
        stack_spec = pl.BlockSpec((depth, tm, width), lambda i: (0, i, 0))
    else:
        stack_spec = pl.BlockSpec((None, tm, width), lambda i: (layer, i, 0))
    if v_transposed:
        vb_shape = jax.ShapeDtypeStruct((rows // tm, width, tm), BF16)
        vb_spec = pl.BlockSpec((1, width, tm), lambda i: (i, 0, 0))
    else:
        vb_shape, vb_spec = bf, row_spec(width)
    resident = functools.partial(pl.BlockSpec, pipeline_mode=pl.Buffered(1))
    in_specs = [row_spec(d),
                pl.BlockSpec((1, d), lambda i: (0, 0)),
                resident((None, d, 4 * width), lambda i: (layer, 0, 0)),
                tab_spec, tab_spec]
    args = [y, norm_w, w_in_bf, cos_t, sin_t]
    aliases = {}
    out_specs = [row_spec(width), stack_spec, stack_spec, row_spec(width), vb_spec, row_spec(width)]
    out_shape = [bf, stack_shape, stack_shape, bf, vb_shape, bf]
    if prev is not None:
        aliases = {len(args): 1, len(args) + 1: 2}
        args += list(prev)
        in_specs += [pl.BlockSpec(memory_space=pl.ANY)] * 2
        in_specs += [row_spec(width), resident((None, width, d), lambda i: (layer - 1, 0, 0))]
        out_specs.append(row_spec(d))
        out_shape.append(jax.ShapeDtypeStruct((rows, d), F32))
    return pl.pallas_call(
        functools.partial(_proj_kernel, rope=rope, q_scale=q_scale, width=width,
                          v_transposed=v_transposed, fused=prev is not None),
        grid=(rows // tm,),
        in_specs=in_specs,
        out_specs=out_specs,
        out_shape=out_shape,
        input_output_aliases=aliases,
        compiler_params=pltpu.CompilerParams(
            dimension_semantics=("arbitrary",), vmem_limit_bytes=V7X_VMEM_LIMIT),
        name="in_proj",
    )(*args)


def _out_kernel(o_ref, w_ref, y_ref, fw_ref, out_ref):
    out_ref[...] = _rms(y_ref[...] + _dot(o_ref[...], w_ref[...]), fw_ref[...])


def _out_project(o_bf, w_out_bf, layer, y, final_w, *, tm):
    rows, d = y.shape
    width = o_bf.shape[1]
    return pl.pallas_call(
        _out_kernel,
        grid=(rows // tm,),
        in_specs=[pl.BlockSpec((tm, width), lambda i: (i, 0)),
                  pl.BlockSpec((None, width, d), lambda i: (layer, 0, 0)),
                  pl.BlockSpec((tm, d), lambda i: (i, 0)),
                  pl.BlockSpec((1, d), lambda i: (0, 0))],
        out_specs=pl.BlockSpec((tm, d), lambda i: (i, 0)),
        out_shape=jax.ShapeDtypeStruct((rows, d), F32),
        compiler_params=pltpu.CompilerParams(
            dimension_semantics=("arbitrary",), vmem_limit_bytes=V7X_VMEM_LIMIT),
        name="out_proj",
    )(o_bf, w_out_bf, y, final_w)


def _diff_lambda(lam_ref, lam_init):
    lf = lam_ref[...]
    a = jnp.sum(lf[0:1] * lf[1:2], axis=-1, keepdims=True)
    b = jnp.sum(lf[2:3] * lf[3:4], axis=-1, keepdims=True)
    return jnp.exp(a) - jnp.exp(b) + lam_init


def _diff_finish(o, sw, gate, lam_init):
    o = _rms(o, sw) * (1.0 - lam_init)
    return (o * gate).astype(BF16)


def _neg_abs(x):
    bits = lax.bitcast_convert_type(x, jnp.uint32) | jnp.uint32(0x80000000)
    return lax.bitcast_convert_type(bits, F32)


def _break_logs(z):
    lg = jnp.log(1.0 + jnp.exp(-jnp.abs(z)))
    return jnp.maximum(z, 0.0) + lg, jnp.minimum(z, 0.0) - lg


def _later_ones(n):
    j = lax.broadcasted_iota(jnp.int32, (n, n), 0)
    s = lax.broadcasted_iota(jnp.int32, (n, n), 1)
    return (j > s).astype(BF16)


def _diff_prompt_kernel(q_ref, k_ref, vt_ref, g_ref, lam_ref, sw_ref, o_ref,
                        s_ref, p_ref, mx_ref, al_ref, m_ref, l_ref, acc_ref, *, tile, lam_init, heads):
    qi = pl.program_id(2)
    maps = [(h, i) for h in range(heads) for i in range(2)]
    lanes = lambda h, i: slice(h * HEAD_W + i * DIFF_DH, h * HEAD_W + (i + 1) * DIFF_DH)
    qm = [q_ref[0, :, lanes(h, i)] for h, i in maps]
    m_ref[...] = jnp.full(m_ref.shape, NEG_INF, F32)
    l_ref[...] = jnp.zeros(l_ref.shape, F32)
    acc_ref[...] = jnp.zeros(acc_ref.shape, F32)

    def scores(kb):
        start = pl.multiple_of(kb * tile, tile)
        for n, (h, i) in enumerate(maps):
            s = _nt_dot(k_ref[0, pl.ds(start, tile), lanes(h, i)], qm[n])
            s_ref[n] = s
            mx_ref[n] = jnp.max(s, axis=0, keepdims=True)

    def softmax(visible):
        for n in range(len(maps)):
            s = s_ref[n]
            if visible is None:
                mx = mx_ref[n]
            else:
                s = jnp.where(visible, s, NEG_INF)
                mx = jnp.max(s, axis=0, keepdims=True)
            m_old = m_ref[n]
            m_new = jnp.maximum(m_old, mx)
            p = jnp.exp2(s - m_new)
            alpha = jnp.exp2(m_old - m_new)
            l_ref[n] = alpha * l_ref[n] + jnp.sum(p, axis=0, keepdims=True)
            p_ref[n] = p.astype(BF16)
            al_ref[n] = alpha
            m_ref[n] = m_new

    def values(kb):
        for n, (h, i) in enumerate(maps):
            vt = vt_ref[kb, h * HEAD_W:(h + 1) * HEAD_W, :]
            acc_ref[n] = al_ref[n] * acc_ref[n] + _dot(vt, p_ref[n])

    key = lax.broadcasted_iota(jnp.int32, (tile, tile), 0)
    qry = lax.broadcasted_iota(jnp.int32, (tile, tile), 1)
    scores(qi)
    softmax((key // CHUNK) <= (qry // CHUNK))
    scores(0)

    def step(i):
        values(jnp.where(i == 0, qi, i - 1))
        softmax(None)
        scores(jnp.minimum(i + 1, qi))

    def pair(j, carry):
        step(2 * j)
        step(2 * j + 1)
        return carry

    def single(i, carry):
        step(i)
        return carry

    pairs = lax.shift_right_logical(qi, 1)
    lax.fori_loop(0, pairs, pair, 0)
    lax.fori_loop(2 * pairs, qi, single, 0)
    values(jnp.maximum(qi - 1, 0))

    lam = _diff_lambda(lam_ref, lam_init)
    for h in range(heads):
        cols = slice(h * HEAD_W, (h + 1) * HEAD_W)
        ot = acc_ref[2 * h] / l_ref[2 * h] - lam * (acc_ref[2 * h + 1] / l_ref[2 * h + 1])
        o_ref[0, :, cols] = _diff_finish(ot.T, sw_ref[...], g_ref[0, :, cols].astype(F32), lam_init)


def _sb_prompt_kernel(q_ref, k_ref, vt_ref, g_ref, o_ref, z_ref, a_ref, c_ref, acc_ref,
                      *, tile, sub, heads):
    qi = pl.program_id(2)
    cols = lambda h: slice(h * HEAD_W, (h + 1) * HEAD_W)
    q = [q_ref[0, :, cols(h)] for h in range(heads)]
    j = lax.broadcasted_iota(jnp.int32, (sub, sub), 0)
    s = lax.broadcasted_iota(jnp.int32, (sub, sub), 1)
    upper = (s > j).astype(BF16)
    c_ref[...] = jnp.zeros(c_ref.shape, F32)
    acc_ref[...] = jnp.zeros(acc_ref.shape, F32)

    def logits(kb):
        start = pl.multiple_of(kb * tile, tile)
        for h in range(heads):
            z_ref[h] = _nt_dot(k_ref[0, pl.ds(start, tile), cols(h)], q[h])

    def weights(diagonal):
        for h in range(heads):
            c = c_ref[h]
            for d in reversed(range(tile // sub)):
                z = z_ref[h, d * sub:(d + 1) * sub, :]
                sp = jnp.maximum(z, 0.0) + jnp.log2(1.0 + jnp.exp2(_neg_abs(z)))
                ls = z - sp
                if diagonal:
                    key = lax.broadcasted_iota(jnp.int32, (sub, tile), 0) + d * sub
                    qry = lax.broadcasted_iota(jnp.int32, (sub, tile), 1)
                    earlier = key < qry
                    sp = jnp.where(earlier, sp, 0.0)
                sp = sp.astype(BF16)
                w = _dot(upper, sp)
                a = jnp.exp2(ls - w - c)
                if diagonal:
                    a = jnp.where(earlier, a, 0.0)
                a_ref[h, d * sub:(d + 1) * sub, :] = a.astype(BF16)
                c = c + w[0:1, :] + sp[0:1, :].astype(F32)
            c_ref[h] = c

    def values(kb):
        for h in range(heads):
            acc_ref[h] += _dot(vt_ref[kb, cols(h), :], a_ref[h])

    def more_weight_possible():
        return (jnp.min(c_ref[...]) < SB_ZERO_MASS).astype(jnp.int32)

    logits(qi)
    weights(True)
    logits(jnp.maximum(qi - 1, 0))

    def cond(state):
        i, go = state
        return jnp.logical_and(i < qi, go > 0)

    def body(state):
        i, _ = state
        kb = qi - 1 - i
        values(kb + 1)
        weights(False)
        logits(jnp.maximum(kb - 1, 0))
        return i + 1, more_weight_possible()

    visited, _ = lax.while_loop(cond, body, (jnp.int32(0), jnp.int32(1)))
    values(qi - visited)
    for h in range(heads):
        o_ref[0, :, cols(h)] = (acc_ref[h].T * g_ref[0, :, cols(h)].astype(F32)).astype(BF16)


def _prompt_attention(q, k, vt, g, lam_vecs, subln_w, *, diff, lam_init, tile):
    b, t, width = q.shape
    heads = math.gcd(PROMPT_HEADS, width // HEAD_W)
    cols = heads * HEAD_W
    nk = t // tile
    vt = vt.reshape(b * nk, width, tile)
    resident = functools.partial(pl.BlockSpec, pipeline_mode=pl.Buffered(1))
    q_tile = pl.BlockSpec((1, tile, cols), lambda bi, hi, qi: (bi, qi, hi))
    k_full = resident((1, t, cols), lambda bi, hi, qi: (bi, 0, hi))
    vt_full = resident((nk, cols, tile), lambda bi, hi, qi: (bi, hi, 0))
    params = pltpu.CompilerParams(dimension_semantics=("arbitrary",) * 3,
                                  vmem_limit_bytes=V7X_VMEM_LIMIT)
    out_shape = jax.ShapeDtypeStruct((b, t, width), BF16)
    grid = (b, width // cols, nk)
    if diff:
        stat = pltpu.VMEM((2 * heads, 1, tile), F32)
        return pl.pallas_call(
            functools.partial(_diff_prompt_kernel, tile=tile, lam_init=lam_init, heads=heads),
            grid=grid,
            in_specs=[q_tile, k_full, vt_full, q_tile,
                      pl.BlockSpec((4, DIFF_DH), lambda bi, hi, qi: (0, 0)),
                      pl.BlockSpec((1, HEAD_W), lambda bi, hi, qi: (0, 0))],
            out_specs=q_tile, out_shape=out_shape,
            scratch_shapes=[pltpu.VMEM((2 * heads, tile, tile), F32),
                            pltpu.VMEM((2 * heads, tile, tile), BF16),
                            stat, stat, stat, stat, pltpu.VMEM((2 * heads, HEAD_W, tile), F32)],
            compiler_params=params, name="diff_prompt",
        )(q, k, vt, g, lam_vecs, subln_w)
    sub = math.gcd(SB_SUB, tile)
    return pl.pallas_call(
        functools.partial(_sb_prompt_kernel, tile=tile, sub=sub, heads=heads),
        grid=grid,
        in_specs=[q_tile, k_full, vt_full, q_tile],
        out_specs=q_tile, out_shape=out_shape,
        scratch_shapes=[pltpu.VMEM((heads, tile, tile), F32), pltpu.VMEM((heads, tile, tile), BF16),
                        pltpu.VMEM((heads, 1, tile), F32), pltpu.VMEM((heads, HEAD_W, tile), F32)],
        compiler_params=params, name="sb_prompt",
    )(q, k, vt, g)


def _diff_sample_kernel(q_ref, kn_ref, vn_ref, kc_ref, vc_ref, g_ref, lam_ref, sw_ref, o_ref,
                        *, past, lam_init, heads):
    t = q_ref.shape[1]
    row = lax.broadcasted_iota(jnp.int32, (t, t), 0)
    col = lax.broadcasted_iota(jnp.int32, (t, t), 1)
    visible = ((past + col) // CHUNK) <= ((past + row) // CHUNK)
    lam = _diff_lambda(lam_ref, lam_init)

    for h in range(heads):
        cols = slice(h * HEAD_W, (h + 1) * HEAD_W)
        q = q_ref[0, :, cols]
        kn = kn_ref[0, :, cols]
        vn = vn_ref[0, :, cols]
        kc = kc_ref[0, 0, :, cols].astype(BF16)
        vc = vc_ref[0, 0, :, cols].astype(BF16)

        def one_map(lo):
            qm = q[:, lo:lo + DIFF_DH]
            sp = _nt_dot(qm, kc[:, lo:lo + DIFF_DH])
            sn = jnp.where(visible, _nt_dot(qm, kn[:, lo:lo + DIFF_DH]), NEG_INF)
            m = jnp.maximum(jnp.max(sp, axis=-1, keepdims=True), jnp.max(sn, axis=-1, keepdims=True))
            pp = jnp.exp(sp - m)
            pn = jnp.exp(sn - m)
            l = jnp.sum(pp, axis=-1, keepdims=True) + jnp.sum(pn, axis=-1, keepdims=True)
            return (_dot(pp.astype(BF16), vc) + _dot(pn.astype(BF16), vn)) / l

        o = one_map(0) - lam * one_map(DIFF_DH)
        o_ref[0, :, cols] = _diff_finish(o, sw_ref[...], g_ref[0, :, cols].astype(F32), lam_init)


def _sb_sample_kernel(q_ref, kn_ref, vn_ref, kc_ref, vc_ref, g_ref, o_ref, *, past, tk, heads):
    t = q_ref.shape[1]
    row = lax.broadcasted_iota(jnp.int32, (t, t), 0)
    col = lax.broadcasted_iota(jnp.int32, (t, t), 1)
    earlier = col < row
    tri_new = _later_ones(t)
    tri = _later_ones(tk)
    for h in range(heads):
        cols = slice(h * HEAD_W, (h + 1) * HEAD_W)
        q = q_ref[0, :, cols]
        sp, ls = _break_logs(_nt_dot(q, kn_ref[0, :, cols]))
        sp = jnp.where(earlier, sp, 0.0).astype(BF16)
        w = _dot(sp, tri_new)
        a = jnp.where(earlier, jnp.exp(ls - w), 0.0)
        acc = _dot(a.astype(BF16), vn_ref[0, :, cols])
        c = w[:, 0:1] + sp[:, 0:1].astype(F32)
        for kb in reversed(range(past // tk)):
            ks = kc_ref[0, 0, kb * tk:(kb + 1) * tk, cols].astype(BF16)
            vs = vc_ref[0, 0, kb * tk:(kb + 1) * tk, cols].astype(BF16)
            sp, ls = _break_logs(_nt_dot(q, ks))
            sp = sp.astype(BF16)
            w = _dot(sp, tri)
            acc = acc + _dot(jnp.exp(ls - w - c).astype(BF16), vs)
            c = c + w[:, 0:1] + sp[:, 0:1].astype(F32)
        o_ref[0, :, cols] = (acc * g_ref[0, :, cols].astype(F32)).astype(BF16)


def _sample_attention(q, kn, vn, cache_k, cache_v, layer, g, lam_vecs, subln_w, *, diff, lam_init):
    b, t, width = q.shape
    past = cache_k.shape[2]
    heads = math.gcd(SAMPLE_HEADS, width // HEAD_W)
    cols = heads * HEAD_W
    tile = pl.BlockSpec((1, t, cols), lambda bi, hi: (bi, 0, hi))
    cache = pl.BlockSpec((1, 1, past, cols), lambda bi, hi: (layer, bi, 0, hi))
    params = pltpu.CompilerParams(dimension_semantics=("arbitrary",) * 2,
                                  vmem_limit_bytes=V7X_VMEM_LIMIT)
    out_shape = jax.ShapeDtypeStruct((b, t, width), BF16)
    grid = (b, width // cols)
    if diff:
        return pl.pallas_call(
            functools.partial(_diff_sample_kernel, past=past, lam_init=lam_init, heads=heads),
            grid=grid,
            in_specs=[tile, tile, tile, cache, cache, tile,
                      pl.BlockSpec((4, DIFF_DH), lambda bi, hi: (0, 0)),
                      pl.BlockSpec((1, HEAD_W), lambda bi, hi: (0, 0))],
            out_specs=tile, out_shape=out_shape, compiler_params=params, name="diff_sample",
        )(q, kn, vn, cache_k, cache_v, g, lam_vecs, subln_w)
    tk = math.gcd(past, V7X_MXU_DEPTH)
    return pl.pallas_call(
        functools.partial(_sb_sample_kernel, past=past, tk=tk, heads=heads),
        grid=grid,
        in_specs=[tile, tile, tile, cache, cache, tile],
        out_specs=tile, out_shape=out_shape, compiler_params=params, name="sb_sample",
    )(q, kn, vn, cache_k, cache_v, g)


def _rope_tables(positions):
    half = DIFF_DH // 2
    inv = ROPE_THETA ** (-jnp.arange(half, dtype=F32) / half)
    ang = positions.astype(F32)[:, None] * inv[None, :]
    cos = jnp.cos(ang)
    sin = jnp.sin(ang)
    return jnp.concatenate([cos, cos], axis=1), jnp.concatenate([-sin, sin], axis=1)


def kernel(x_prompt, x_sample, cache_k, cache_v, norm_w, w_in, w_out, diff_lambda, diff_subln_w,
           final_norm_w):
    depth, d_model = norm_w.shape
    b, t, _ = x_prompt.shape
    sb, st, _ = x_sample.shape
    past = cache_k.shape[2]
    width = w_out.shape[1]
    assert width == DIFF_HEADS * 2 * DIFF_DH == SB_HEADS * SB_DH
    assert past % CHUNK == 0 and st % 16 == 0

    tile = min(ATTN_TILE, t)
    tm_p = tile
    tm_s = min(ROW_TILE, sb * st)
    assert t % tile == 0 and tile % CHUNK == 0 and (sb * st) % tm_s == 0 and tm_s % st == 0

    cos_p, sin_p = _rope_tables(jnp.arange(t, dtype=jnp.int32))
    cos_s, sin_s = _rope_tables(past + jnp.arange(st, dtype=jnp.int32))
    cos_s = jnp.tile(cos_s, (tm_s // st, 1))
    sin_s = jnp.tile(sin_s, (tm_s // st, 1))

    w_in_bf = w_in.astype(BF16)
    w_out_bf = w_out.astype(BF16)
    final_w = final_norm_w.reshape(1, d_model)

    yp = x_prompt.reshape(b * t, d_model)
    ys = x_sample.reshape(sb * st, d_model)
    stacks_p = stacks_s = ()
    op = os_ = None
    for i in range(depth):
        diff = i % 2 == 0
        j = i // 2
        lam_init = 0.8 - 0.6 * math.exp(-0.3 * i)
        q_scale = (DIFF_DH if diff else SB_DH) ** -0.5
        nw = norm_w[i].reshape(1, d_model)
        lam_vecs = diff_lambda[j]
        subln = diff_subln_w[j].reshape(1, HEAD_W)

        res = _project(yp, nw, w_in_bf, cos_p, sin_p, i, depth,
                       None if i == 0 else (*stacks_p, op, w_out_bf),
                       rope=diff, q_scale=q_scale * LOG2E, tm=tm_p, v_transposed=True)
        q, kst, vst, kb, vt, g = res[:6]
        yp = yp if i == 0 else res[6]
        stacks_p = (kst, vst)
        shp = (b, t, width)
        op = _prompt_attention(q.reshape(shp), kb.reshape(shp), vt, g.reshape(shp), lam_vecs, subln,
                               diff=diff, lam_init=lam_init, tile=tile).reshape(b * t, width)

        res = _project(ys, nw, w_in_bf, cos_s, sin_s, i, depth,
                       None if i == 0 else (*stacks_s, os_, w_out_bf),
                       rope=diff, q_scale=q_scale, tm=tm_s, v_transposed=False)
        q, kst, vst, kb, vb, g = res[:6]
        ys = ys if i == 0 else res[6]
        stacks_s = (kst, vst)
        shp = (sb, st, width)
        os_ = _sample_attention(q.reshape(shp), kb.reshape(shp), vb.reshape(shp), cache_k, cache_v, i,
                                g.reshape(shp), lam_vecs, subln, diff=diff,
                                lam_init=lam_init).reshape(sb * st, width)

    yp = _out_project(op, w_out_bf, depth - 1, yp, final_w, tm=tm_p)
    ys = _out_project(os_, w_out_bf, depth - 1, ys, final_w, tm=tm_s)
    return (yp.reshape(b, t, d_model), ys.reshape(sb, st, d_model),
            stacks_p[0].reshape(depth, b, t, width), stacks_p[1].reshape(depth, b, t, width),
            stacks_s[0].reshape(depth, sb, st, width), stacks_s[1].reshape(depth, sb, st, width))
```

```python
import functools
import math

import jax
import jax.numpy as jnp
from jax import lax
from jax.experimental import pallas as pl
from jax.experimental.pallas import tpu as pltpu

F32 = jnp.float32
BF16 = jnp.bfloat16

CHUNK = 64
DIFF_HEADS = 4
DIFF_DH = 128
SB_HEADS = 4
SB_DH = 256
HEAD_W = 256
ROPE_THETA = 10000.0
EPS = 1e-6
NEG_INF = -1e30
LOG2E = math.log2(math.e)

V7X_VMEM_LIMIT = 56 * 1024 * 1024
V7X_MXU_DEPTH = 256
ROW_TILE = 512
ATTN_TILE = 512
SB_SUB = V7X_MXU_DEPTH
SB_ZERO_MASS = float("inf")
SAMPLE_HEADS = 4
PROMPT_HEADS = 2


def _nt_dot(a, b):
    return lax.dot_general(a, b, (((1,), (1,)), ((), ())), preferred_element_type=F32)


def _dot(a, b):
    return jnp.dot(a, b, preferred_element_type=F32)


def _rms(x, w):
    return x * lax.rsqrt(jnp.mean(x * x, axis=-1, keepdims=True) + EPS) * w


def _rope_tile(x, cos, sin_signed):
    outs = []
    for g in range(x.shape[1] // DIFF_DH):
        xg = x[:, g * DIFF_DH:(g + 1) * DIFF_DH]
        outs.append(xg * cos + pltpu.roll(xg, DIFF_DH // 2, axis=1) * sin_signed)
    return jnp.concatenate(outs, axis=1)


def _proj_kernel(y_ref, nw_ref, w_ref, cos_ref, sin_ref, *refs,
                 rope, q_scale, width, v_transposed, fused):
    if fused:
        _, _, o_ref, wout_ref, *refs = refs
    q_ref, kf_ref, vf_ref, kb_ref, vb_ref, g_ref = refs[:6]
    y = y_ref[...]
    if fused:
        y = y + _dot(o_ref[...], wout_ref[...])
        refs[6][...] = y
    h = _rms(y, nw_ref[...]).astype(BF16)
    q = _dot(h, w_ref[:, 0 * width:1 * width])
    k = _dot(h, w_ref[:, 1 * width:2 * width])
    v = _dot(h, w_ref[:, 2 * width:3 * width])
    g = _dot(h, w_ref[:, 3 * width:4 * width])
    if rope:
        cos = cos_ref[...]
        sin = sin_ref[...]
        q = _rope_tile(q, cos, sin)
        k = _rope_tile(k, cos, sin)
    q_ref[...] = (q * q_scale).astype(BF16)
    kb_ref[...] = k.astype(BF16)
    if fused:
        kf_ref[...] = k
        vf_ref[...] = v
    else:
        for st_ref, rows in ((kf_ref, k), (vf_ref, v)):
            st_ref[0] = rows
            st_ref[1:] = jnp.zeros((st_ref.shape[0] - 1,) + rows.shape, F32)
    if v_transposed:
        vb_ref[0] = v.T.astype(BF16)
    else:
        vb_ref[...] = v.astype(BF16)
    g_ref[...] = (g * jax.nn.sigmoid(g)).astype(BF16)


def _project(y, norm_w, w_in_bf, cos_t, sin_t, layer, depth, prev, *,
             rope, q_scale, tm, v_transposed):
    rows, d = y.shape
    width = w_in_bf.shape[2] // 4
    n_pos_tiles = cos_t.shape[0] // tm
    row_spec = lambda w: pl.BlockSpec((tm, w), lambda i: (i, 0))
    tab_spec = pl.BlockSpec((tm, DIFF_DH), lambda i: (i % n_pos_tiles, 0))
    bf = jax.ShapeDtypeStruct((rows, width), BF16)
    stack_shape = jax.ShapeDtypeStruct((depth, rows, width), F32)
    if prev is None:
        stack_spec = pl.BlockSpec((depth, tm, width), lambda i: (0, i, 0))
    else:
        stack_spec = pl.BlockSpec((None, tm, width), lambda i: (layer, i, 0))
    if v_transposed:
        vb_shape = jax.ShapeDtypeStruct((rows // tm, width, tm), BF16)
        vb_spec = pl.BlockSpec((1, width, tm), lambda i: (i, 0, 0))
    else:
        vb_shape, vb_spec = bf, row_spec(width)
    resident = functools.partial(pl.BlockSpec, pipeline_mode=pl.Buffered(1))
    in_specs = [row_spec(d),
                pl.BlockSpec((1, d), lambda i: (0, 0)),
                resident((None, d, 4 * width), lambda i: (layer, 0, 0)),
                tab_spec, tab_spec]
    args = [y, norm_w, w_in_bf, cos_t, sin_t]
    aliases = {}
    out_specs = [row_spec(width), stack_spec, stack_spec, row_spec(width), vb_spec, row_spec(width)]
    out_shape = [bf, stack_shape, stack_shape, bf, vb_shape, bf]
    if prev is not None:
        aliases = {len(args): 1, len(args) + 1: 2}
        args += list(prev)
        in_specs += [pl.BlockSpec(memory_space=pl.ANY)] * 2
        in_specs += [row_spec(width), resident((None, width, d), lambda i: (layer - 1, 0, 0))]
        out_specs.append(row_spec(d))
        out_shape.append(jax.ShapeDtypeStruct((rows, d), F32))
    return pl.pallas_call(
        functools.partial(_proj_kernel, rope=rope, q_scale=q_scale, width=width,
                          v_transposed=v_transposed, fused=prev is not None),
        grid=(rows // tm,),
        in_specs=in_specs,
        out_specs=out_specs,
        out_shape=out_shape,
        input_output_aliases=aliases,
        compiler_params=pltpu.CompilerParams(
            dimension_semantics=("arbitrary",), vmem_limit_bytes=V7X_VMEM_LIMIT),
        name="in_proj",
    )(*args)


def _out_kernel(o_ref, w_ref, y_ref, fw_ref, out_ref):
    out_ref[...] = _rms(y_ref[...] + _dot(o_ref[...], w_ref[...]), fw_ref[...])


def _out_project(o_bf, w_out_bf, layer, y, final_w, *, tm):
    rows, d = y.shape
    width = o_bf.shape[1]
    return pl.pallas_call(
        _out_kernel,
        grid=(rows // tm,),
        in_specs=[pl.BlockSpec((tm, width), lambda i: (i, 0)),
                  pl.BlockSpec((None, width, d), lambda i: (layer, 0, 0)),
                  pl.BlockSpec((tm, d), lambda i: (i, 0)),
                  pl.BlockSpec((1, d), lambda i: (0, 0))],
        out_specs=pl.BlockSpec((tm, d), lambda i: (i, 0)),
        out_shape=jax.ShapeDtypeStruct((rows, d), F32),
        compiler_params=pltpu.CompilerParams(
            dimension_semantics=("arbitrary",), vmem_limit_bytes=V7X_VMEM_LIMIT),
        name="out_proj",
    )(o_bf, w_out_bf, y, final_w)


def _diff_lambda(lam_ref, lam_init):
    lf = lam_ref[...]
    a = jnp.sum(lf[0:1] * lf[1:2], axis=-1, keepdims=True)
    b = jnp.sum(lf[2:3] * lf[3:4], axis=-1, keepdims=True)
    return jnp.exp(a) - jnp.exp(b) + lam_init


def _diff_finish(o, sw, gate, lam_init):
    o = _rms(o, sw) * (1.0 - lam_init)
    return (o * gate).astype(BF16)


def _neg_abs(x):
    bits = lax.bitcast_convert_type(x, jnp.uint32) | jnp.uint32(0x80000000)
    return lax.bitcast_convert_type(bits, F32)


def _break_logs(z):
    lg = jnp.log(1.0 + jnp.exp(-jnp.abs(z)))
    return jnp.maximum(z, 0.0) + lg, jnp.minimum(z, 0.0) - lg


def _later_ones(n):
    j = lax.broadcasted_iota(jnp.int32, (n, n), 0)
    s = lax.broadcasted_iota(jnp.int32, (n, n), 1)
    return (j > s).astype(BF16)


def _diff_prompt_kernel(q_ref, k_ref, vt_ref, g_ref, lam_ref, sw_ref, o_ref,
                        s_ref, p_ref, mx_ref, al_ref, m_ref, l_ref, acc_ref, *, tile, lam_init, heads):
    qi = pl.program_id(2)
    maps = [(h, i) for h in range(heads) for i in range(2)]
    lanes = lambda h, i: slice(h * HEAD_W + i * DIFF_DH, h * HEAD_W + (i + 1) * DIFF_DH)
    qm = [q_ref[0, :, lanes(h, i)] for h, i in maps]
    m_ref[...] = jnp.full(m_ref.shape, NEG_INF, F32)
    l_ref[...] = jnp.zeros(l_ref.shape, F32)
    acc_ref[...] = jnp.zeros(acc_ref.shape, F32)

    def scores(kb):
        start = pl.multiple_of(kb * tile, tile)
        for n, (h, i) in enumerate(maps):
            s = _nt_dot(k_ref[0, pl.ds(start, tile), lanes(h, i)], qm[n])
            s_ref[n] = s
            mx_ref[n] = jnp.max(s, axis=0, keepdims=True)

    def softmax(visible):
        for n in range(len(maps)):
            s = s_ref[n]
            if visible is None:
                mx = mx_ref[n]
            else:
                s = jnp.where(visible, s, NEG_INF)
                mx = jnp.max(s, axis=0, keepdims=True)
            m_old = m_ref[n]
            m_new = jnp.maximum(m_old, mx)
            p = jnp.exp2(s - m_new)
            alpha = jnp.exp2(m_old - m_new)
            l_ref[n] = alpha * l_ref[n] + jnp.sum(p, axis=0, keepdims=True)
            p_ref[n] = p.astype(BF16)
            al_ref[n] = alpha
            m_ref[n] = m_new

    def values(kb):
        for n, (h, i) in enumerate(maps):
            vt = vt_ref[kb, h * HEAD_W:(h + 1) * HEAD_W, :]
            acc_ref[n] = al_ref[n] * acc_ref[n] + _dot(vt, p_ref[n])

    key = lax.broadcasted_iota(jnp.int32, (tile, tile), 0)
    qry = lax.broadcasted_iota(jnp.int32, (tile, tile), 1)
    scores(qi)
    softmax((key // CHUNK) <= (qry // CHUNK))
    scores(0)

    def step(i):
        values(jnp.where(i == 0, qi, i - 1))
        softmax(None)
        scores(jnp.minimum(i + 1, qi))

    def pair(j, carry):
        step(2 * j)
        step(2 * j + 1)
        return carry

    def single(i, carry):
        step(i)
        return carry

    pairs = lax.shift_right_logical(qi, 1)
    lax.fori_loop(0, pairs, pair, 0)
    lax.fori_loop(2 * pairs, qi, single, 0)
    values(jnp.maximum(qi - 1, 0))

    lam = _diff_lambda(lam_ref, lam_init)
    for h in range(heads):
        cols = slice(h * HEAD_W, (h + 1) * HEAD_W)
        ot = acc_ref[2 * h] / l_ref[2 * h] - lam * (acc_ref[2 * h + 1] / l_ref[2 * h + 1])
        o_ref[0, :, cols] = _diff_finish(ot.T, sw_ref[...], g_ref[0, :, cols].astype(F32), lam_init)


def _sb_prompt_kernel(q_ref, k_ref, vt_ref, g_ref, o_ref, z_ref, a_ref, c_ref, acc_ref,
                      *, tile, sub, heads):
    qi = pl.program_id(2)
    cols = lambda h: slice(h * HEAD_W, (h + 1) * HEAD_W)
    q = [q_ref[0, :, cols(h)] for h in range(heads)]
    j = lax.broadcasted_iota(jnp.int32, (sub, sub), 0)
    s = lax.broadcasted_iota(jnp.int32, (sub, sub), 1)
    upper = (s > j).astype(BF16)
    c_ref[...] = jnp.zeros(c_ref.shape, F32)
    acc_ref[...] = jnp.zeros(acc_ref.shape, F32)

    def logits(kb):
        start = pl.multiple_of(kb * tile, tile)
        for h in range(heads):
            z_ref[h] = _nt_dot(k_ref[0, pl.ds(start, tile), cols(h)], q[h])

    def weights(diagonal):
        for h in range(heads):
            c = c_ref[h]
            for d in reversed(range(tile // sub)):
                z = z_ref[h, d * sub:(d + 1) * sub, :]
                sp = jnp.maximum(z, 0.0) + jnp.log2(1.0 + jnp.exp2(_neg_abs(z)))
                ls = z - sp
                if diagonal:
                    key = lax.broadcasted_iota(jnp.int32, (sub, tile), 0) + d * sub
                    qry = lax.broadcasted_iota(jnp.int32, (sub, tile), 1)
                    earlier = key < qry
                    sp = jnp.where(earlier, sp, 0.0)
                sp = sp.astype(BF16)
                w = _dot(upper, sp)
                a = jnp.exp2(ls - w - c)
                if diagonal:
                    a = jnp.where(earlier, a, 0.0)
                a_ref[h, d * sub:(d + 1) * sub, :] = a.astype(BF16)
                c = c + w[0:1, :] + sp[0:1, :].astype(F32)
            c_ref[h] = c

    def values(kb):
        for h in range(heads):
            acc_ref[h] += _dot(vt_ref[kb, cols(h), :], a_ref[h])

    def more_weight_possible():
        return (jnp.min(c_ref[...]) < SB_ZERO_MASS).astype(jnp.int32)

    logits(qi)
    weights(True)
    logits(jnp.maximum(qi - 1, 0))

    def cond(state):
        i, go = state
        return jnp.logical_and(i < qi, go > 0)

    def body(state):
        i, _ = state
        kb = qi - 1 - i
        values(kb + 1)
        weights(False)
        logits(jnp.maximum(kb - 1, 0))
        return i + 1, more_weight_possible()

    visited, _ = lax.while_loop(cond, body, (jnp.int32(0), jnp.int32(1)))
    values(qi - visited)
    for h in range(heads):
        o_ref[0, :, cols(h)] = (acc_ref[h].T * g_ref[0, :, cols(h)].astype(F32)).astype(BF16)


def _prompt_attention(q, k, vt, g, lam_vecs, subln_w, *, diff, lam_init, tile):
    b, t, width = q.shape
    heads = math.gcd(PROMPT_HEADS, width // HEAD_W)
    cols = heads * HEAD_W
    nk = t // tile
    vt = vt.reshape(b * nk, width, tile)
    resident = functools.partial(pl.BlockSpec, pipeline_mode=pl.Buffered(1))
    q_tile = pl.BlockSpec((1, tile, cols), lambda bi, hi, qi: (bi, qi, hi))
    k_full = resident((1, t, cols), lambda bi, hi, qi: (bi, 0, hi))
    vt_full = resident((nk, cols, tile), lambda bi, hi, qi: (bi, hi, 0))
    params = pltpu.CompilerParams(dimension_semantics=("arbitrary",) * 3,
                                  vmem_limit_bytes=V7X_VMEM_LIMIT)
    out_shape = jax.ShapeDtypeStruct((b, t, width), BF16)
    grid = (b, width // cols, nk)
    if diff:
        stat = pltpu.VMEM((2 * heads, 1, tile), F32)
        return pl.pallas_call(
            functools.partial(_diff_prompt_kernel, tile=tile, lam_init=lam_init, heads=heads),
            grid=grid,
            in_specs=[q_tile, k_full, vt_full, q_tile,
                      pl.BlockSpec((4, DIFF_DH), lambda bi, hi, qi: (0, 0)),
                      pl.BlockSpec((1, HEAD_W), lambda bi, hi, qi: (0, 0))],
            out_specs=q_tile, out_shape=out_shape,
            scratch_shapes=[pltpu.VMEM((2 * heads, tile, tile), F32),
                            pltpu.VMEM((2 * heads, tile, tile), BF16),
                            stat, stat, stat, stat, pltpu.VMEM((2 * heads, HEAD_W, tile), F32)],
            compiler_params=params, name="diff_prompt",
        )(q, k, vt, g, lam_vecs, subln_w)
    sub = math.gcd(SB_SUB, tile)
    return pl.pallas_call(
        functools.partial(_sb_prompt_kernel, tile=tile, sub=sub, heads=heads),
        grid=grid,
        in_specs=[q_tile, k_full, vt_full, q_tile],
        out_specs=q_tile, out_shape=out_shape,
        scratch_shapes=[pltpu.VMEM((heads, tile, tile), F32), pltpu.VMEM((heads, tile, tile), BF16),
                        pltpu.VMEM((heads, 1, tile), F32), pltpu.VMEM((heads, HEAD_W, tile), F32)],
        compiler_params=params, name="sb_prompt",
    )(q, k, vt, g)


def _diff_sample_kernel(q_ref, kn_ref, vn_ref, kc_ref, vc_ref, g_ref, lam_ref, sw_ref, o_ref,
                        *, past, lam_init, heads):
    t = q_ref.shape[1]
    row = lax.broadcasted_iota(jnp.int32, (t, t), 0)
    col = lax.broadcasted_iota(jnp.int32, (t, t), 1)
    visible = ((past + col) // CHUNK) <= ((past + row) // CHUNK)
    lam = _diff_lambda(lam_ref, lam_init)

    for h in range(heads):
        cols = slice(h * HEAD_W, (h + 1) * HEAD_W)
        q = q_ref[0, :, cols]
        kn = kn_ref[0, :, cols]
        vn = vn_ref[0, :, cols]
        kc = kc_ref[0, 0, :, cols].astype(BF16)
        vc = vc_ref[0, 0, :, cols].astype(BF16)

        def one_map(lo):
            qm = q[:, lo:lo + DIFF_DH]
            sp = _nt_dot(qm, kc[:, lo:lo + DIFF_DH])
            sn = jnp.where(visible, _nt_dot(qm, kn[:, lo:lo + DIFF_DH]), NEG_INF)
            m = jnp.maximum(jnp.max(sp, axis=-1, keepdims=True), jnp.max(sn, axis=-1, keepdims=True))
            pp = jnp.exp(sp - m)
            pn = jnp.exp(sn - m)
            l = jnp.sum(pp, axis=-1, keepdims=True) + jnp.sum(pn, axis=-1, keepdims=True)
            return (_dot(pp.astype(BF16), vc) + _dot(pn.astype(BF16), vn)) / l

        o = one_map(0) - lam * one_map(DIFF_DH)
        o_ref[0, :, cols] = _diff_finish(o, sw_ref[...], g_ref[0, :, cols].astype(F32), lam_init)


def _sb_sample_kernel(q_ref, kn_ref, vn_ref, kc_ref, vc_ref, g_ref, o_ref, *, past, tk, heads):
    t = q_ref.shape[1]
    row = lax.broadcasted_iota(jnp.int32, (t, t), 0)
    col = lax.broadcasted_iota(jnp.int32, (t, t), 1)
    earlier = col < row
    tri_new = _later_ones(t)
    tri = _later_ones(tk)
    for h in range(heads):
        cols = slice(h * HEAD_W, (h + 1) * HEAD_W)
        q = q_ref[0, :, cols]
        sp, ls = _break_logs(_nt_dot(q, kn_ref[0, :, cols]))
        sp = jnp.where(earlier, sp, 0.0).astype(BF16)
        w = _dot(sp, tri_new)
        a = jnp.where(earlier, jnp.exp(ls - w), 0.0)
        acc = _dot(a.astype(BF16), vn_ref[0, :, cols])
        c = w[:, 0:1] + sp[:, 0:1].astype(F32)
        for kb in reversed(range(past // tk)):
            ks = kc_ref[0, 0, kb * tk:(kb + 1) * tk, cols].astype(BF16)
            vs = vc_ref[0, 0, kb * tk:(kb + 1) * tk, cols].astype(BF16)
            sp, ls = _break_logs(_nt_dot(q, ks))
            sp = sp.astype(BF16)
            w = _dot(sp, tri)
            acc = acc + _dot(jnp.exp(ls - w - c).astype(BF16), vs)
            c = c + w[:, 0:1] + sp[:, 0:1].astype(F32)
        o_ref[0, :, cols] = (acc * g_ref[0, :, cols].astype(F32)).astype(BF16)


def _sample_attention(q, kn, vn, cache_k, cache_v, layer, g, lam_vecs, subln_w, *, diff, lam_init):
    b, t, width = q.shape
    past = cache_k.shape[2]
    heads = math.gcd(SAMPLE_HEADS, width // HEAD_W)
    cols = heads * HEAD_W
    tile = pl.BlockSpec((1, t, cols), lambda bi, hi: (bi, 0, hi))
    cache = pl.BlockSpec((1, 1, past, cols), lambda bi, hi: (layer, bi, 0, hi))
    params = pltpu.CompilerParams(dimension_semantics=("arbitrary",) * 2,
                                  vmem_limit_bytes=V7X_VMEM_LIMIT)
    out_shape = jax.ShapeDtypeStruct((b, t, width), BF16)
    grid = (b, width // cols)
    if diff:
        return pl.pallas_call(
            functools.partial(_diff_sample_kernel, past=past, lam_init=lam_init, heads=heads),
            grid=grid,
            in_specs=[tile, tile, tile, cache, cache, tile,
                      pl.BlockSpec((4, DIFF_DH), lambda bi, hi: (0, 0)),
                      pl.BlockSpec((1, HEAD_W), lambda bi, hi: (0, 0))],
            out_specs=tile, out_shape=out_shape, compiler_params=params, name="diff_sample",
        )(q, kn, vn, cache_k, cache_v, g, lam_vecs, subln_w)
    tk = math.gcd(past, V7X_MXU_DEPTH)
    return pl.pallas_call(
        functools.partial(_sb_sample_kernel, past=past, tk=tk, heads=heads),
        grid=grid,
        in_specs=[tile, tile, tile, cache, cache, tile],
        out_specs=tile, out_shape=out_shape, compiler_params=params, name="sb_sample",
    )(q, kn, vn, cache_k, cache_v, g)


def _rope_tables(positions):
    half = DIFF_DH // 2
    inv = ROPE_THETA ** (-jnp.arange(half, dtype=F32) / half)
    ang = positions.astype(F32)[:, None] * inv[None, :]
    cos = jnp.cos(ang)
    sin = jnp.sin(ang)
    return jnp.concatenate([cos, cos], axis=1), jnp.concatenate([-sin, sin], axis=1)


def kernel(x_prompt, x_sample, cache_k, cache_v, norm_w, w_in, w_out, diff_lambda, diff_subln_w,
           final_norm_w):
    depth, d_model = norm_w.shape
    b, t, _ = x_prompt.shape
    sb, st, _ = x_sample.shape
    past = cache_k.shape[2]
    width = w_out.shape[1]
    assert width == DIFF_HEADS * 2 * DIFF_DH == SB_HEADS * SB_DH
    assert past % CHUNK == 0 and st % 16 == 0

    tile = min(ATTN_TILE, t)
    tm_p = tile
    tm_s = min(ROW_TILE, sb * st)
    assert t % tile == 0 and tile % CHUNK == 0 and (sb * st) % tm_s == 0 and tm_s % st == 0

    cos_p, sin_p = _rope_tables(jnp.arange(t, dtype=jnp.int32))
    cos_s, sin_s = _rope_tables(past + jnp.arange(st, dtype=jnp.int32))
    cos_s = jnp.tile(cos_s, (tm_s // st, 1))
    sin_s = jnp.tile(sin_s, (tm_s // st, 1))

    w_in_bf = w_in.astype(BF16)
    w_out_bf = w_out.astype(BF16)
    final_w = final_norm_w.reshape(1, d_model)

    yp = x_prompt.reshape(b * t, d_model)
    ys = x_sample.reshape(sb * st, d_model)
    stacks_p = stacks_s = ()
    op = os_ = None
    for i in range(depth):
        diff = i % 2 == 0
        j = i // 2
        lam_init = 0.8 - 0.6 * math.exp(-0.3 * i)
        q_scale = (DIFF_DH if diff else SB_DH) ** -0.5
        nw = norm_w[i].reshape(1, d_model)
        lam_vecs = diff_lambda[j]
        subln = diff_subln_w[j].reshape(1, HEAD_W)

        res = _project(yp, nw, w_in_bf, cos_p, sin_p, i, depth,
                       None if i == 0 else (*stacks_p, op, w_out_bf),
                       rope=diff, q_scale=q_scale * LOG2E, tm=tm_p, v_transposed=True)
        q, kst, vst, kb, vt, g = res[:6]
        yp = yp if i == 0 else res[6]
        stacks_p = (kst, vst)
        shp = (b, t, width)
        op = _prompt_attention(q.reshape(shp), kb.reshape(shp), vt, g.reshape(shp), lam_vecs, subln,
                               diff=diff, lam_init=lam_init, tile=tile).reshape(b * t, width)

        res = _project(ys, nw, w_in_bf, cos_s, sin_s, i, depth,
                       None if i == 0 else (*stacks_s, os_, w_out_bf),
                       rope=diff, q_scale=q_scale, tm=tm_s, v_transposed=False)
        q, kst, vst, kb, vb, g = res[:6]
        ys = ys if i == 0 else res[6]
        stacks_s = (kst, vst)
        shp = (sb, st, width)
        os_ = _sample_attention(q.reshape(shp), kb.reshape(shp), vb.reshape(shp), cache_k, cache_v, i,
                                g.reshape(shp), lam_vecs, subln, diff=diff,
                                lam_init=lam_init).reshape(sb * st, width)

    yp = _out_project(op, w_out_bf, depth - 1, yp, final_w, tm=tm_p)
    ys = _out_project(os_, w_out_bf, depth - 1, ys, final_w, tm=tm_s)
    return (yp.reshape(b, t, d_model), ys.reshape(sb, st, d_model),
            stacks_p[0].reshape(depth, b, t, width), stacks_p[1].reshape(depth, b, t, width),
            stacks_s[0].reshape(depth, sb, st, width), stacks_s[1].reshape(depth, sb, st, width))
```
